```python
import math
import jax, jax.numpy as jnp
from jax import lax
import numpy as np

D_MODEL = 2048
BATCH = 8
SEQ = 2048
DEPTH = 1
DEC_BATCH = 32
DEC_SEQ = 1
PAST_LEN = 16384
PAGE_SIZE = 128

N_HEADS = 8
N_KV_HEADS = 2
HEAD_DIM = 128
ATTN_WIDTH = N_HEADS * HEAD_DIM
KV_WIDTH = N_KV_HEADS * HEAD_DIM
IDX_HEADS = 16
IDX_DIM = 64
TOPK_MAX = 256
Q_BLOCK = 128
SSM_WIDTH = D_MODEL - ATTN_WIDTH
SSM_CH = 16
SSM_GROUPS = SSM_WIDTH // SSM_CH
SSM_STATE = 64
DT_MIN = 0.001
DT_MAX = 0.1
PEER_HEADS = 8
PEER_KEY_DIM = 256
PEER_HALF = PEER_KEY_DIM // 2
N_KEYS = 128
N_EXPERTS = N_KEYS * N_KEYS
PEER_TOPK = 16
TOK_BLOCK = 128
RMS_EPS = 1e-6
IN_SIZES = (ATTN_WIDTH, KV_WIDTH, KV_WIDTH, IDX_HEADS * IDX_DIM, IDX_DIM, IDX_HEADS, SSM_WIDTH)
IN_COLS = ATTN_WIDTH + 2 * KV_WIDTH + IDX_HEADS * IDX_DIM + IDX_DIM + IDX_HEADS + SSM_WIDTH

kernel_name = 'hymba_dsa_s5_peer_step'


def rms_norm(x, g):
    xf = x.astype(jnp.float32)
    y = xf * lax.rsqrt(jnp.mean(xf * xf, axis=-1, keepdims=True) + RMS_EPS)
    return (y * g.astype(jnp.float32)).astype(x.dtype)


def adaln(c, w_ada, b_ada):
    m = jax.nn.silu(c) @ w_ada + b_ada
    return jnp.split(m, 6, axis=-1)


def modulate(h, shift, scale):
    return h * (1.0 + scale[:, None, :]) + shift[:, None, :]


def input_projection(h, w_in, q_gain, k_gain):
    b, l = h.shape[:2]
    p = h @ w_in
    cuts = [int(v) for v in np.cumsum(np.array(IN_SIZES))[:-1]]
    q, k, v, qi, ki, wi, u = jnp.split(p, cuts, axis=-1)
    q = rms_norm(q.reshape(b, l, N_HEADS, HEAD_DIM), q_gain)
    k = rms_norm(k.reshape(b, l, N_KV_HEADS, HEAD_DIM), k_gain)
    v = v.reshape(b, l, N_KV_HEADS, HEAD_DIM)
    qi = qi.reshape(b, l, IDX_HEADS, IDX_DIM)
    return q, k, v, qi, ki, wi, u


def indexer_scores(q_idx, w_idx, k_idx):
    r = jax.nn.relu(jnp.einsum('bthd,bsd->bths', q_idx, k_idx).astype(jnp.float32))
    return jnp.einsum('bths,bth->bts', r, w_idx.astype(jnp.float32))


def select_keys(scores, q_pos, k_pos, topk):
    admissible = k_pos[None, None, :] <= q_pos[None, :, None]
    masked = jnp.where(admissible, scores, -jnp.inf)
    _, sel = lax.top_k(masked, topk)
    valid = sel <= q_pos[None, :, None]
    return sel, valid


def sparse_attend(q, k_sel, v_sel, valid):
    b, t = q.shape[:2]
    qg = q.reshape(b, t, N_KV_HEADS, N_HEADS // N_KV_HEADS, HEAD_DIM)
    s = jnp.einsum('bthgd,btshd->bthgs', qg, k_sel).astype(jnp.float32) * (HEAD_DIM ** -0.5)
    s = jnp.where(valid[:, :, None, None, :], s, -jnp.inf)
    p = jax.nn.softmax(s, axis=-1).astype(v_sel.dtype)
    o = jnp.einsum('bthgs,btshd->bthgd', p, v_sel)
    return o.reshape(b, t, ATTN_WIDTH)


def gather_rows(rows, idx):
    return jax.vmap(lambda r, i: r[i])(rows, idx)


def prompt_attention(q, k, v, q_idx, k_idx, w_idx):
    b, s = q.shape[:2]
    topk = min(TOPK_MAX, s // 4)
    nb = s // Q_BLOCK
    k_pos = jnp.arange(s)

    def to_blocks(a):
        return jnp.moveaxis(a.reshape(b, nb, Q_BLOCK, *a.shape[2:]), 1, 0)

    def block(args):
        qb, qib, wb, q_pos = args
        sc = indexer_scores(qib, wb, k_idx)
        sel, valid = select_keys(sc, q_pos, k_pos, topk)
        return sparse_attend(qb, gather_rows(k, sel), gather_rows(v, sel), valid)

    q_pos = jnp.arange(s).reshape(nb, Q_BLOCK)
    out = lax.map(block, (to_blocks(q), to_blocks(q_idx), to_blocks(w_idx), q_pos))
    return jnp.moveaxis(out, 0, 1).reshape(b, s, ATTN_WIDTH)


def sample_attention(q, k_new, v_new, q_idx, k_idx_new, w_idx, cache_k, cache_v, cache_kidx, page_table):
    db, ds = q.shape[:2]
    past = page_table.shape[1] * PAGE_SIZE
    total = past + ds
    topk = min(TOPK_MAX, total // 4)
    k_idx_past = cache_kidx[page_table].reshape(db, past, IDX_DIM)
    k_idx_all = jnp.concatenate([k_idx_past, k_idx_new.astype(k_idx_past.dtype)], axis=1)
    sc = indexer_scores(q_idx, w_idx, k_idx_all)
    q_pos = past + jnp.arange(ds)
    sel, valid = select_keys(sc, q_pos, jnp.arange(total), topk)
    in_past = sel < past
    ps = jnp.minimum(sel, past - 1)
    phys = jax.vmap(lambda pt, i: pt[i])(page_table, ps // PAGE_SIZE)
    off = ps % PAGE_SIZE
    ns = jnp.clip(sel - past, 0, ds - 1)
    cond = in_past[..., None, None]
    k_sel = jnp.where(cond, cache_k[phys, off], gather_rows(k_new, ns).astype(cache_k.dtype))
    v_sel = jnp.where(cond, cache_v[phys, off], gather_rows(v_new, ns).astype(cache_v.dtype))
    return sparse_attend(q.astype(k_sel.dtype), k_sel, v_sel, valid).astype(q.dtype)


def s5_mixer(u, h0_re, h0_im, a_re, a_im, b_re, b_im, c_re, c_im, d_skip, log_dt, w_glu, b_glu):
    f32 = jnp.float32
    bsz, l = u.shape[:2]
    uf = u.astype(f32).reshape(bsz, l, SSM_GROUPS, SSM_CH)
    a_re = a_re.astype(f32); a_im = a_im.astype(f32)
    dt = jnp.exp(log_dt.astype(f32))[:, None]
    mag = jnp.exp(a_re * dt)
    ab_re = mag * jnp.cos(a_im * dt)
    ab_im = mag * jnp.sin(a_im * dt)
    den = a_re * a_re + a_im * a_im
    nr = ab_re - 1.0
    co_re = (nr * a_re + ab_im * a_im) / den
    co_im = (ab_im * a_re - nr * a_im) / den
    b_re = b_re.astype(f32); b_im = b_im.astype(f32)
    bb_re = co_re[..., None] * b_re - co_im[..., None] * b_im
    bb_im = co_re[..., None] * b_im + co_im[..., None] * b_re
    bu_re = jnp.einsum('gpc,blgc->blgp', bb_re, uf)
    bu_im = jnp.einsum('gpc,blgc->blgp', bb_im, uf)
    ar = jnp.broadcast_to(ab_re, bu_re.shape)
    ai = jnp.broadcast_to(ab_im, bu_re.shape)

    def combine(e1, e2):
        a1r, a1i, b1r, b1i = e1
        a2r, a2i, b2r, b2i = e2
        return (a2r * a1r - a2i * a1i, a2r * a1i + a2i * a1r,
                a2r * b1r - a2i * b1i + b2r, a2r * b1i + a2i * b1r + b2i)

    acc_re, acc_im, hz_re, hz_im = lax.associative_scan(combine, (ar, ai, bu_re, bu_im), axis=1)
    h0r = h0_re.astype(f32)[:, None]; h0i = h0_im.astype(f32)[:, None]
    h_re = hz_re + acc_re * h0r - acc_im * h0i
    h_im = hz_im + acc_re * h0i + acc_im * h0r
    y = (jnp.einsum('gcp,blgp->blgc', c_re.astype(f32), h_re)
         - jnp.einsum('gcp,blgp->blgc', c_im.astype(f32), h_im)
         + d_skip.astype(f32) * uf)
    y = jax.nn.gelu(y).reshape(bsz, l, SSM_WIDTH)
    y = y * jax.nn.sigmoid(y @ w_glu.astype(f32) + b_glu.astype(f32))
    return y.astype(u.dtype), h_re[:, -1], h_im[:, -1]


def peer_block(x, w_q, sub_keys1, sub_keys2, u_tab, v_tab):
    t = x.shape[0]
    q = (x @ w_q).reshape(t, PEER_HEADS, 2, PEER_HALF)
    s1 = jnp.einsum('thd,hnd->thn', q[:, :, 0], sub_keys1).astype(jnp.float32)
    s2 = jnp.einsum('thd,hnd->thn', q[:, :, 1], sub_keys2).astype(jnp.float32)
    v1, i1 = lax.top_k(s1, PEER_TOPK)
    v2, i2 = lax.top_k(s2, PEER_TOPK)
    cand = (v1[..., :, None] + v2[..., None, :]).reshape(t, PEER_HEADS, PEER_TOPK * PEER_TOPK)
    sc, ci = lax.top_k(cand, PEER_TOPK)
    e = (jnp.take_along_axis(i1, ci // PEER_TOPK, axis=-1) * N_KEYS
         + jnp.take_along_axis(i2, ci % PEER_TOPK, axis=-1))
    g = jax.nn.softmax(sc, axis=-1)
    hid = jax.nn.gelu(jnp.einsum('td,thkd->thk', x, u_tab[e]).astype(jnp.float32))
    return jnp.einsum('thk,thkd->td', (g * hid).astype(x.dtype), v_tab[e])


def peer_ffn(x, w_q, sub_keys1, sub_keys2, u_tab, v_tab):
    b, l, d = x.shape
    t = b * l
    blk = min(TOK_BLOCK, t)
    nb = -(-t // blk)
    xf = jnp.pad(x.reshape(t, d), ((0, nb * blk - t), (0, 0)))
    out = lax.map(lambda xb: peer_block(xb, w_q, sub_keys1, sub_keys2, u_tab, v_tab),
                  xf.reshape(nb, blk, d))
    return out.reshape(nb * blk, d)[:t].reshape(b, l, d)


def trunk_layer(x, c, attend, h0_re, h0_im, wts):
    (w_ada, b_ada, norm1_g, w_in, q_gain, k_gain, ssm_params,
     attn_out_g, ssm_out_g, w_out, norm2_g, peer_params) = wts
    shift1, scale1, gate1, shift2, scale2, gate2 = adaln(c, w_ada, b_ada)
    h = modulate(rms_norm(x, norm1_g), shift1, scale1)
    q, k, v, q_idx, k_idx, w_idx, u = input_projection(h, w_in, q_gain, k_gain)
    a = attend(q, k, v, q_idx, k_idx, w_idx)
    s, h_re, h_im = s5_mixer(u, h0_re, h0_im, *ssm_params)
    mix = jnp.concatenate([rms_norm(a, attn_out_g), rms_norm(s, ssm_out_g)], axis=-1) @ w_out
    x = x + gate1[:, None, :] * mix
    h2 = modulate(rms_norm(x, norm2_g), shift2, scale2)
    x = x + gate2[:, None, :] * peer_ffn(h2, *peer_params)
    return x, k, v, k_idx, h_re, h_im


def setup_inputs(seed: int = 0) -> dict:
    key = jax.random.key(seed)
    ks = jax.random.split(key, 40)
    f32 = jnp.float32
    nrm = lambda k, shape, s: jax.random.normal(k, shape, f32) * s
    n_pages = PAST_LEN // PAGE_SIZE
    n_used = DEC_BATCH * n_pages
    n_pool = n_used + n_used // 4
    page_table = jax.random.permutation(ks[5], n_pool)[:n_used].reshape(DEC_BATCH, n_pages).astype(jnp.int32)
    a_im = jnp.pi * jnp.arange(SSM_STATE, dtype=f32)[None, :] + nrm(ks[15], (SSM_GROUPS, SSM_STATE), 0.01)
    return {
        'x_prompt': nrm(ks[0], (BATCH, SEQ, D_MODEL), 1.0),
        'x_sample': nrm(ks[1], (DEC_BATCH, DEC_SEQ, D_MODEL), 1.0),
        'cache_k': nrm(ks[2], (n_pool, PAGE_SIZE, N_KV_HEADS, HEAD_DIM), 1.0),
        'cache_v': nrm(ks[3], (n_pool, PAGE_SIZE, N_KV_HEADS, HEAD_DIM), 1.0),
        'cache_kidx': nrm(ks[4], (n_pool, PAGE_SIZE, IDX_DIM), 1.0),
        'state_ssm_re': nrm(ks[6], (DEC_BATCH, SSM_GROUPS, SSM_STATE), 0.1),
        'state_ssm_im': nrm(ks[7], (DEC_BATCH, SSM_GROUPS, SSM_STATE), 0.1),
        'page_table': page_table,
        'c_prompt': nrm(ks[8], (BATCH, D_MODEL), 1.0),
        'c_sample': nrm(ks[9], (DEC_BATCH, D_MODEL), 1.0),
        'w_ada': nrm(ks[10], (D_MODEL, 6 * D_MODEL), 0.01),
        'b_ada': nrm(ks[11], (6 * D_MODEL,), 0.02),
        'norm1_g': 1.0 + nrm(ks[12], (D_MODEL,), 0.02),
        'w_in': nrm(ks[13], (D_MODEL, IN_COLS), D_MODEL ** -0.5),
        'q_gain': 1.0 + nrm(ks[14], (HEAD_DIM,), 0.02),
        'k_gain': 1.0 + nrm(ks[16], (HEAD_DIM,), 0.02),
        'ssm_A_re': -0.5 + nrm(ks[17], (SSM_GROUPS, SSM_STATE), 0.01),
        'ssm_A_im': a_im,
        'ssm_B_re': nrm(ks[18], (SSM_GROUPS, SSM_STATE, SSM_CH), (2 * SSM_CH) ** -0.5),
        'ssm_B_im': nrm(ks[19], (SSM_GROUPS, SSM_STATE, SSM_CH), (2 * SSM_CH) ** -0.5),
        'ssm_C_re': nrm(ks[20], (SSM_GROUPS, SSM_CH, SSM_STATE), (SSM_STATE) ** -0.5),
        'ssm_C_im': nrm(ks[21], (SSM_GROUPS, SSM_CH, SSM_STATE), (SSM_STATE) ** -0.5),
        'ssm_D': nrm(ks[22], (SSM_GROUPS, SSM_CH), 1.0),
        'ssm_log_dt': jax.random.uniform(ks[23], (SSM_GROUPS,), f32, math.log(DT_MIN), math.log(DT_MAX)),
        'ssm_w_glu': nrm(ks[24], (SSM_WIDTH, SSM_WIDTH), SSM_WIDTH ** -0.5),
        'ssm_b_glu': nrm(ks[25], (SSM_WIDTH,), 0.01),
        'attn_out_g': 1.0 + nrm(ks[26], (ATTN_WIDTH,), 0.02),
        'ssm_out_g': 1.0 + nrm(ks[27], (SSM_WIDTH,), 0.02),
        'w_out': nrm(ks[28], (ATTN_WIDTH + SSM_WIDTH, D_MODEL), (ATTN_WIDTH + SSM_WIDTH) ** -0.5),
        'norm2_g': 1.0 + nrm(ks[29], (D_MODEL,), 0.02),
        'peer_w_q': nrm(ks[30], (D_MODEL, PEER_HEADS * PEER_KEY_DIM), D_MODEL ** -0.5),
        'peer_sub_keys1': nrm(ks[31], (PEER_HEADS, N_KEYS, PEER_HALF), PEER_HALF ** -0.5),
        'peer_sub_keys2': nrm(ks[32], (PEER_HEADS, N_KEYS, PEER_HALF), PEER_HALF ** -0.5),
        'peer_u': nrm(ks[33], (N_EXPERTS, D_MODEL), D_MODEL ** -0.5),
        'peer_v': nrm(ks[34], (N_EXPERTS, D_MODEL), PEER_HEADS ** -0.5),
    }


def reference(x_prompt, x_sample, cache_k, cache_v, cache_kidx, state_ssm_re, state_ssm_im, page_table,
              c_prompt, c_sample, w_ada, b_ada, norm1_g, w_in, q_gain, k_gain,
              ssm_A_re, ssm_A_im, ssm_B_re, ssm_B_im, ssm_C_re, ssm_C_im, ssm_D, ssm_log_dt,
              ssm_w_glu, ssm_b_glu, attn_out_g, ssm_out_g, w_out, norm2_g,
              peer_w_q, peer_sub_keys1, peer_sub_keys2, peer_u, peer_v):
    ssm_params = (ssm_A_re, ssm_A_im, ssm_B_re, ssm_B_im, ssm_C_re, ssm_C_im, ssm_D, ssm_log_dt,
                  ssm_w_glu, ssm_b_glu)
    peer_params = (peer_w_q, peer_sub_keys1, peer_sub_keys2, peer_u, peer_v)
    wts = (w_ada, b_ada, norm1_g, w_in, q_gain, k_gain, ssm_params,
           attn_out_g, ssm_out_g, w_out, norm2_g, peer_params)

    def sample_attend(q, k, v, q_idx, k_idx, w_idx):
        return sample_attention(q, k, v, q_idx, k_idx, w_idx, cache_k, cache_v, cache_kidx, page_table)

    xp, xs = x_prompt, x_sample
    h0 = jnp.zeros((x_prompt.shape[0], SSM_GROUPS, SSM_STATE), jnp.float32)
    for _ in range(DEPTH):
        xp, k_p, v_p, kidx_p, re_p, im_p = trunk_layer(xp, c_prompt, prompt_attention, h0, h0, wts)
        xs, k_s, v_s, kidx_s, re_s, im_s = trunk_layer(xs, c_sample, sample_attend,
                                                        state_ssm_re, state_ssm_im, wts)
    return (xp, xs, k_p, v_p, kidx_p, re_p, im_p, k_s, v_s, kidx_s, re_s, im_s)
```

```python
import functools

import jax
import jax.numpy as jnp
from jax import lax
from jax.experimental import pallas as pl
from jax.experimental.pallas import tpu as pltpu

F32 = jnp.float32
BF16 = jnp.bfloat16
I32 = jnp.int32

RMS_EPS = 1e-6
TOPK_MAX = 256
PEER_TOPK = 16
LANES = 128
SUBLANES = 8
SSM_SLAB_GROUPS = 16
INT_MIN = -(2 ** 31)
VMEM_LIMIT = 56 * 1024 * 1024


def _cp(*sem, vmem=VMEM_LIMIT, **kw):
    return pltpu.CompilerParams(dimension_semantics=sem, vmem_limit_bytes=vmem, **kw)


def _nt_dot(a, b):
    return lax.dot_general(a, b, (((1,), (1,)), ((), ())), preferred_element_type=F32)


def _dot(a, b):
    return jnp.dot(a, b, preferred_element_type=F32)


def _rms(x, g):
    return x * lax.rsqrt(jnp.mean(x * x, axis=-1, keepdims=True) + RMS_EPS) * g


def _mod_spec(mod, tm, rows_per_batch):
    _, r, d = mod.shape
    if r == 1:
        return pl.BlockSpec((1, 1, d), lambda i, *_: ((i * tm) // rows_per_batch, 0, 0))
    assert r == tm
    return pl.BlockSpec((1, r, d), lambda i, *_: (0, 0, 0))


def _adaln_kernel(c_ref, w_ref, b_ref, o_ref):
    c = c_ref[...]
    s = (c * jax.nn.sigmoid(c)).astype(BF16)
    o_ref[...] = _dot(s, w_ref[...].astype(BF16)) + b_ref[...]


def _adaln(c, w, b):
    m, d = c.shape
    n = w.shape[1]
    tn = 512
    return pl.pallas_call(
        _adaln_kernel, grid=(n // tn,),
        in_specs=[pl.BlockSpec((m, d), lambda j: (0, 0)), pl.BlockSpec((d, tn), lambda j: (0, j)),
                  pl.BlockSpec((1, tn), lambda j: (0, j))],
        out_specs=pl.BlockSpec((m, tn), lambda j: (0, j)),
        out_shape=jax.ShapeDtypeStruct((m, n), F32), compiler_params=_cp("parallel"))(c, w, b.reshape(1, n))


def _head_rms_store(o_ref, p, gain, hd):
    for h in range(p.shape[1] // hd):
        o_ref[:, h * hd:(h + 1) * hd] = _rms(p[:, h * hd:(h + 1) * hd], gain).astype(o_ref.dtype)


def _proj_qkv_kernel(x_ref, sh_ref, sc_ref, g_ref, wq_ref, wk_ref, wv_ref, qg_ref, kg_ref,
                     q_ref, k_ref, v_ref, *, hd):
    hb = (_rms(x_ref[...], g_ref[...]) * (1.0 + sc_ref[0]) + sh_ref[0]).astype(BF16)
    _head_rms_store(q_ref, _dot(hb, wq_ref[...]), qg_ref[...], hd)
    _head_rms_store(k_ref, _dot(hb, wk_ref[...]), kg_ref[...], hd)
    v_ref[...] = _dot(hb, wv_ref[...])


def _proj_idx_kernel(x_ref, sh_ref, sc_ref, g_ref, wqi_ref, wkw_ref, wu_ref, qi_ref, kw_ref, u_ref):
    hb = (_rms(x_ref[...], g_ref[...]) * (1.0 + sc_ref[0]) + sh_ref[0]).astype(BF16)
    qi_ref[...] = _dot(hb, wqi_ref[...]).astype(qi_ref.dtype)
    kw_ref[...] = _dot(hb, wkw_ref[...])
    u_ref[...] = _dot(hb, wu_ref[...])


def _row_call(kernel, tm, rows_per_batch, x, mods, consts, outs):
    t, d = x.shape
    full = lambda a: pl.BlockSpec(a.shape, lambda i: (0,) * a.ndim)
    return pl.pallas_call(
        kernel, grid=(t // tm,),
        in_specs=[pl.BlockSpec((tm, d), lambda i: (i, 0))] + [_mod_spec(m, tm, rows_per_batch) for m in mods]
        + [full(c) for c in consts],
        out_specs=[pl.BlockSpec((tm, n), lambda i: (i, 0)) for n, _ in outs],
        out_shape=[jax.ShapeDtypeStruct((t, n), dt) for n, dt in outs],
        compiler_params=_cp("parallel"))(x, *mods, *consts)


def _order_key(s):
    bits = lax.bitcast_convert_type(s, I32)
    bits = jnp.where(s == 0.0, 0, bits)
    return jnp.where(bits < 0, bits ^ 0x7FFFFFFF, bits)


def _topk_mask(key, col, k, extra_key=None, ncols=None):
    m = key.shape[0]
    kf = float(k)

    def count_ge(c):
        n = jnp.sum((key >= c).astype(F32), axis=-1, keepdims=True)
        if extra_key is not None:
            n = n + (extra_key >= c).astype(F32)
        return n

    def bit_step(it, v):
        cand = v + lax.shift_left(jnp.int32(1), 31 - it)
        return jnp.where(count_ge(cand) >= kf, cand, v)

    v = lax.fori_loop(0, 32, bit_step, jnp.full((m, 1), INT_MIN, I32))
    gt = key > v
    eq = key == v
    n_gt = jnp.sum(gt.astype(F32), axis=-1, keepdims=True)
    if extra_key is not None:
        n_gt = n_gt + (extra_key > v).astype(F32)
    need = kf - n_gt
    nbits = max(1, (ncols - 1).bit_length())

    def col_step(it, lim):
        cand = lim + lax.shift_left(jnp.int32(1), nbits - 1 - it)
        n = jnp.sum((eq & (col < cand)).astype(F32), axis=-1, keepdims=True)
        return jnp.where(n < need, cand, lim)

    lim = lax.fori_loop(0, nbits, col_step, jnp.zeros((m, 1), I32))
    mask = gt | (eq & (col <= lim))
    if extra_key is None:
        return mask, None
    n_eq = jnp.sum(eq.astype(F32), axis=-1, keepdims=True)
    extra_sel = (extra_key > v) | ((extra_key == v) & (n_eq < need))
    return mask, extra_sel


def _prompt_attn_kernel(q_ref, qi_ref, kwq_ref, k_ref, v_ref, kwall_ref, og_ref, o_ref, acc_ref,
                        *, nh, nkv, hd, ih, idim, topk):
    tq = q_ref.shape[0]
    s_len = k_ref.shape[0]
    i = pl.program_id(1)
    kib = kwall_ref[:, :idim].astype(BF16)
    w = kwq_ref[:, idim:idim + ih]
    qi = qi_ref[...]
    sc = jnp.zeros((tq, s_len), F32)
    for h in range(ih):
        r = _nt_dot(qi[:, h * idim:(h + 1) * idim], kib)
        sc = sc + jnp.maximum(r, 0.0) * w[:, h:h + 1]
    col = lax.broadcasted_iota(I32, (tq, s_len), 1)
    row = lax.broadcasted_iota(I32, (tq, s_len), 0) + i * tq
    adm = col <= row
    key = jnp.where(adm, _order_key(sc), INT_MIN)
    sel, _ = _topk_mask(key, col, topk, ncols=s_len)
    sel = sel & adm
    kb = k_ref[...].astype(BF16)
    vb = v_ref[...].astype(BF16)
    g = nh // nkv
    scale = hd ** -0.5
    for hk in range(nkv):
        kh = kb[:, hk * hd:(hk + 1) * hd]
        vh = vb[:, hk * hd:(hk + 1) * hd]
        for j in range(g):
            h = hk * g + j
            s = _nt_dot(q_ref[:, h * hd:(h + 1) * hd], kh) * scale
            s = jnp.where(sel, s, -jnp.inf)
            p = jnp.exp(s - jnp.max(s, axis=-1, keepdims=True))
            l = jnp.sum(p, axis=-1, keepdims=True)
            acc_ref[:, h * hd:(h + 1) * hd] = _dot(p.astype(BF16), vh) / l
    o_ref[...] = _rms(acc_ref[...], og_ref[...]).astype(o_ref.dtype)


def _prompt_attention(q, qi, kw, k, v, og, *, b, nh, nkv, hd, ih, idim):
    t, a = q.shape
    s_len = t // b
    tq = min(128, s_len)
    nq = s_len // tq
    topk = min(TOPK_MAX, s_len // 4)
    kern = functools.partial(_prompt_attn_kernel, nh=nh, nkv=nkv, hd=hd, ih=ih, idim=idim, topk=topk)
    blk = lambda n: pl.BlockSpec((tq, n), lambda bi, i: (bi * nq + i, 0))
    allk = lambda n: pl.BlockSpec((s_len, n), lambda bi, i: (bi, 0))
    return pl.pallas_call(
        kern, grid=(b, nq),
        in_specs=[blk(a), blk(qi.shape[1]), blk(kw.shape[1]), allk(k.shape[1]), allk(v.shape[1]),
                  allk(kw.shape[1]), pl.BlockSpec((1, a), lambda bi, i: (0, 0))],
        out_specs=blk(a), out_shape=jax.ShapeDtypeStruct((t, a), BF16),
        scratch_shapes=[pltpu.VMEM((tq, a), F32)],
        compiler_params=_cp("parallel", "parallel"))(q, qi, kw, k, v, kw, og)


def _sample_score_kernel(pt_ref, qi_ref, w_ref, kn_ref, kp_ref, o_ref, sn_ref, *, ih, idim):
    j = pl.program_id(1)

    def score(keys):
        r = _nt_dot(qi_ref[0], keys.astype(BF16))
        return jnp.sum(jnp.maximum(r, 0.0) * w_ref[0], axis=0, keepdims=True)

    o_ref[0, pl.ds(j, 1), :] = score(kp_ref[0])

    @pl.when(j == 0)
    def _():
        sn_ref[0] = score(jnp.broadcast_to(kn_ref[0], (LANES, idim)))


def _sample_select_kernel(sc_ref, sn_ref, m_ref, ns_ref, *, topk):
    sc = sc_ref[...]
    db, past = sc.shape
    col = lax.broadcasted_iota(I32, (db, past), 1)
    mask, new_sel = _topk_mask(_order_key(sc), col, topk, extra_key=_order_key(sn_ref[:, :1]), ncols=past)
    m_ref[...] = mask.astype(F32)
    ns_ref[...] = jnp.broadcast_to(new_sel.astype(F32), ns_ref.shape)


def _sample_attn_kernel(pt_ref, q_ref, m_ref, ns_ref, kn_ref, vn_ref, kp_ref, vp_ref, og_ref, o_ref,
                        mx_ref, l_ref, acc_ref, *, nh, nkv, hd):
    j = pl.program_id(1)
    npg = pl.num_programs(1)
    g = nh // nkv
    scale = hd ** -0.5
    qb = q_ref[0].astype(BF16)

    @pl.when(j == 0)
    def _():
        mx_ref[...] = jnp.full(mx_ref.shape, -jnp.inf, F32)
        l_ref[...] = jnp.zeros(l_ref.shape, F32)
        acc_ref[...] = jnp.zeros(acc_ref.shape, F32)

    def per_kv_head(fn, n):
        head_kv = lax.broadcasted_iota(I32, (nh, n), 0) // g
        out = fn(0)
        for hk in range(1, nkv):
            out = jnp.where(head_kv == hk, fn(hk), out)
        return out

    def update(kb, vb, keep):
        n = kb.shape[0]
        s = per_kv_head(lambda hk: _nt_dot(qb, kb[:, hk * hd:(hk + 1) * hd]), n) * scale
        s = jnp.where(keep, s, -jnp.inf)
        m_old = mx_ref[...]
        m_new = jnp.maximum(m_old, jnp.max(s, axis=-1, keepdims=True))
        m_safe = jnp.where(m_new == -jnp.inf, 0.0, m_new)
        alpha = jnp.exp(m_old - m_safe)
        p = jnp.exp(s - m_safe)
        pb = p.astype(BF16)
        l_ref[...] = alpha * l_ref[...] + jnp.sum(p, axis=-1, keepdims=True)
        acc_ref[...] = alpha * acc_ref[...] + per_kv_head(lambda hk: _dot(pb, vb[:, hk * hd:(hk + 1) * hd]), hd)
        mx_ref[...] = m_new

    update(kp_ref[0].astype(BF16), vp_ref[0].astype(BF16), m_ref[0, 0] > 0.0)

    @pl.when(j == npg - 1)
    def _():
        n8 = kn_ref.shape[1]
        first = lax.broadcasted_iota(I32, (1, n8), 1) == 0
        update(kn_ref[0].astype(BF16), vn_ref[0].astype(BF16), first & (ns_ref[0, :1, :1] > 0.0))
        o = acc_ref[...] / l_ref[...]
        ms = jnp.sum(jnp.sum(o * o, axis=-1, keepdims=True), axis=0, keepdims=True) / (nh * hd)
        o_ref[0] = (o * lax.rsqrt(ms + RMS_EPS) * og_ref[...]).astype(o_ref.dtype)


def _sample_attention(q, qi, kw, k, v, og, cache_k, cache_v, cache_kidx, page_table, *, nh, nkv, hd, ih, idim):
    db = q.shape[0]
    npg = page_table.shape[1]
    page = cache_k.shape[1]
    past = npg * page
    topk = min(TOPK_MAX, (past + 1) // 4)
    kvw = nkv * hd
    qi3 = qi.reshape(db, ih, idim)
    w3 = kw[:, idim:idim + ih].reshape(db, ih, 1)
    scores, s_new = pl.pallas_call(
        functools.partial(_sample_score_kernel, ih=ih, idim=idim),
        grid_spec=pltpu.PrefetchScalarGridSpec(
            num_scalar_prefetch=1, grid=(db, npg),
            in_specs=[pl.BlockSpec((1, ih, idim), lambda b, j, pt: (b, 0, 0)),
                      pl.BlockSpec((1, ih, 1), lambda b, j, pt: (b, 0, 0)),
                      pl.BlockSpec((1, 1, idim), lambda b, j, pt: (b, 0, 0)),
                      pl.BlockSpec((1, page, idim), lambda b, j, pt: (pt[b, j], 0, 0))],
            out_specs=[pl.BlockSpec((1, npg, page), lambda b, j, pt: (b, 0, 0)),
                       pl.BlockSpec((1, 1, LANES), lambda b, j, pt: (b, 0, 0))]),
        out_shape=[jax.ShapeDtypeStruct((db, npg, page), F32), jax.ShapeDtypeStruct((db, 1, LANES), F32)],
        compiler_params=_cp("parallel", "arbitrary"))(page_table, qi3, w3, kw[:, :idim].reshape(db, 1, idim),
                                                      cache_kidx)
    mask, new_sel = pl.pallas_call(
        functools.partial(_sample_select_kernel, topk=topk),
        out_shape=[jax.ShapeDtypeStruct((db, past), F32), jax.ShapeDtypeStruct((db, LANES), F32)],
        compiler_params=_cp())(scores.reshape(db, past), s_new.reshape(db, LANES))
    rep = lambda a: jnp.broadcast_to(a[:, None, :], (db, SUBLANES, a.shape[1]))
    out = pl.pallas_call(
        functools.partial(_sample_attn_kernel, nh=nh, nkv=nkv, hd=hd),
        grid_spec=pltpu.PrefetchScalarGridSpec(
            num_scalar_prefetch=1, grid=(db, npg),
            in_specs=[pl.BlockSpec((1, nh, hd), lambda b, j, pt: (b, 0, 0)),
                      pl.BlockSpec((1, 1, 1, page), lambda b, j, pt: (b, j, 0, 0)),
                      pl.BlockSpec((1, SUBLANES, LANES), lambda b, j, pt: (b, 0, 0)),
                      pl.BlockSpec((1, SUBLANES, kvw), lambda b, j, pt: (b, 0, 0)),
                      pl.BlockSpec((1, SUBLANES, kvw), lambda b, j, pt: (b, 0, 0)),
                      pl.BlockSpec((1, page, kvw), lambda b, j, pt: (pt[b, j], 0, 0)),
                      pl.BlockSpec((1, page, kvw), lambda b, j, pt: (pt[b, j], 0, 0)),
                      pl.BlockSpec((nh, hd), lambda b, j, pt: (0, 0))],
            out_specs=pl.BlockSpec((1, nh, hd), lambda b, j, pt: (b, 0, 0)),
            scratch_shapes=[pltpu.VMEM((nh, 1), F32), pltpu.VMEM((nh, 1), F32), pltpu.VMEM((nh, hd), F32)]),
        out_shape=jax.ShapeDtypeStruct((db, nh, hd), BF16),
        compiler_params=_cp("parallel", "arbitrary"))(
            page_table, q.astype(F32).reshape(db, nh, hd), mask.reshape(db, npg, 1, page), rep(new_sel), rep(k), rep(v),
            cache_k.reshape(-1, page, kvw), cache_v.reshape(-1, page, kvw), og.reshape(nh, hd))
    return out.reshape(db, nh * hd)


def _ssm_param_kernel(are_ref, aim_ref, ldt_ref, bre_ref, bim_ref, abre_ref, abim_ref, bbre_ref, bbim_ref):
    a_re = are_ref[...]
    a_im = aim_ref[...]
    dt = jnp.exp(ldt_ref[...])
    mag = jnp.exp(a_re * dt)
    ab_re = mag * jnp.cos(a_im * dt)
    ab_im = mag * jnp.sin(a_im * dt)
    den = a_re * a_re + a_im * a_im
    nr = ab_re - 1.0
    co_re = (nr * a_re + ab_im * a_im) / den
    co_im = (ab_im * a_re - nr * a_im) / den
    abre_ref[...] = ab_re
    abim_ref[...] = ab_im
    b_re = bre_ref[...]
    b_im = bim_ref[...]
    bbre_ref[...] = co_re[None] * b_re - co_im[None] * b_im
    bbim_ref[...] = co_re[None] * b_im + co_im[None] * b_re


def _s5_kernel(u_ref, h0re_ref, h0im_ref, ar_ref, ai_ref, wre_ref, wim_ref, cre_ref, cim_ref, d_ref,
               wglu_ref, bglu_ref, og_ref, s_ref, hre_ref, him_ref, bure_ref, buim_ref, y_ref, *, r, tc):
    ci = pl.program_id(0)
    ns, sc_w, sp_w = wre_ref.shape

    @pl.when(ci == 0)
    def _():
        hre_ref[...] = h0re_ref[...]
        him_ref[...] = h0im_ref[...]

    for s in range(ns):
        cols = slice(s * sc_w, (s + 1) * sc_w)
        st = slice(s * sp_w, (s + 1) * sp_w)
        ub = u_ref[:, cols].astype(BF16)
        bure_ref[...] = _dot(ub, wre_ref[s])
        buim_ref[...] = _dot(ub, wim_ref[s])
        a_r = jnp.broadcast_to(ar_ref[:, st], (r, sp_w))
        a_i = jnp.broadcast_to(ai_ref[:, st], (r, sp_w))

        def step(t, carry):
            hr, hi = carry
            rows = pl.ds(pl.multiple_of(t * r, r), r)
            nr = a_r * hr - a_i * hi + bure_ref[rows, :]
            ni = a_r * hi + a_i * hr + buim_ref[rows, :]
            bure_ref[rows, :] = nr
            buim_ref[rows, :] = ni
            return nr, ni

        hr, hi = lax.fori_loop(0, tc, step, (hre_ref[:, st], him_ref[:, st]))
        hre_ref[:, st] = hr
        him_ref[:, st] = hi
        y_ref[:, cols] = (_dot(bure_ref[...].astype(BF16), cre_ref[s])
                          - _dot(buim_ref[...].astype(BF16), cim_ref[s]))
    y = jax.nn.gelu(y_ref[...] + d_ref[...] * u_ref[...])
    z = _dot(y.astype(BF16), wglu_ref[...]) + bglu_ref[...]
    s_ref[...] = _rms(y * jax.nn.sigmoid(z), og_ref[...]).astype(s_ref.dtype)


def _s5(u_tb, h0_re, h0_im, ab_re, ab_im, w_re, w_im, c_re, c_im, d, w_glu, b_glu, og, *, r, tc):
    rows, w = u_tb.shape
    gp = h0_re.shape[1]
    full = lambda a: pl.BlockSpec(a.shape, lambda i: (0,) * a.ndim)
    consts = (h0_re, h0_im, ab_re, ab_im, w_re, w_im, c_re, c_im, d, w_glu, b_glu, og)
    sp_w = w_re.shape[2]
    return pl.pallas_call(
        functools.partial(_s5_kernel, r=r, tc=tc), grid=(rows // (r * tc),),
        in_specs=[pl.BlockSpec((r * tc, w), lambda i: (i, 0))] + [full(c) for c in consts],
        out_specs=[pl.BlockSpec((r * tc, w), lambda i: (i, 0)), full(h0_re), full(h0_im)],
        out_shape=[jax.ShapeDtypeStruct((rows, w), BF16), jax.ShapeDtypeStruct((r, gp), F32),
                   jax.ShapeDtypeStruct((r, gp), F32)],
        scratch_shapes=[pltpu.VMEM((r * tc, sp_w), F32), pltpu.VMEM((r * tc, sp_w), F32),
                        pltpu.VMEM((r * tc, w), F32)],
        compiler_params=_cp("arbitrary"))(u_tb, *consts)


def _out_proj_kernel(x_ref, g1_ref, sh_ref, sc_ref, a_ref, s_ref, wa_ref, ws_ref, n2_ref, x1_ref, h2_ref):
    mix = _dot(a_ref[...], wa_ref[...]) + _dot(s_ref[...], ws_ref[...])
    x1 = x_ref[...] + g1_ref[0] * mix
    x1_ref[...] = x1
    h2_ref[...] = _rms(x1, n2_ref[...]) * (1.0 + sc_ref[0]) + sh_ref[0]


def _out_proj(x, gate1, shift2, scale2, a_n, s_n, w_a, w_s, n2, *, tm, rows_per_batch):
    t, d = x.shape
    full = lambda a: pl.BlockSpec(a.shape, lambda i: (0,) * a.ndim)
    rowblk = lambda n: pl.BlockSpec((tm, n), lambda i: (i, 0))
    return pl.pallas_call(
        _out_proj_kernel, grid=(t // tm,),
        in_specs=[rowblk(d)] + [_mod_spec(m, tm, rows_per_batch) for m in (gate1, shift2, scale2)]
        + [rowblk(a_n.shape[1]), rowblk(s_n.shape[1]), full(w_a), full(w_s), full(n2)],
        out_specs=[rowblk(d), rowblk(d)],
        out_shape=[jax.ShapeDtypeStruct((t, d), F32), jax.ShapeDtypeStruct((t, d), F32)],
        compiler_params=_cp("parallel"))(x, gate1, shift2, scale2, a_n, s_n, w_a, w_s, n2)


def _take_top(s, pay, k):
    n, tb = s.shape
    rid = lax.broadcasted_iota(I32, (n, tb), 0).astype(F32)
    kid = lax.broadcasted_iota(I32, (k, tb), 0)
    vals = jnp.zeros((k, tb), F32)
    pays = jnp.zeros((k, tb), F32)
    for it in range(k):
        m = jnp.max(s, axis=0, keepdims=True)
        idx = jnp.min(jnp.where(s == m, rid, float(n)), axis=0, keepdims=True)
        hit = rid == idx
        pv = jnp.max(jnp.where(hit, pay, -1.0), axis=0, keepdims=True)
        vals = jnp.where(kid == it, m, vals)
        pays = jnp.where(kid == it, pv, pays)
        s = jnp.where(hit, -jnp.inf, s)
    return vals, pays


def _peer_route_kernel(h_ref, wq_ref, k1_ref, k2_ref, e_ref, g_ref, *, nk):
    half = k1_ref.shape[2]
    tb = h_ref.shape[0]
    k = PEER_TOPK
    qt = _nt_dot(wq_ref[...], h_ref[...].astype(BF16))
    s1 = _dot(k1_ref[0], qt[:half, :].astype(BF16))
    s2 = _dot(k2_ref[0], qt[half:, :].astype(BF16))
    rid = lax.broadcasted_iota(I32, (nk, tb), 0).astype(F32)
    v1, i1 = _take_top(s1, rid, k)
    v2, i2 = _take_top(s2, rid, k)
    cand = jnp.concatenate([v1[a:a + 1, :] + v2 for a in range(k)], axis=0)
    ids = jnp.concatenate([i1[a:a + 1, :] * float(nk) + i2 for a in range(k)], axis=0)
    sc, e = _take_top(cand, ids, k)
    p = jnp.exp(sc - jnp.max(sc, axis=0, keepdims=True))
    g_ref[0] = p / jnp.sum(p, axis=0, keepdims=True)
    e_ref[0] = e.astype(I32)


def _peer_route(h2, wq_t, k1, k2, *, tb):
    t, d = h2.shape
    hp, nk, half = k1.shape
    return pl.pallas_call(
        functools.partial(_peer_route_kernel, nk=nk), grid=(t // tb, hp),
        in_specs=[pl.BlockSpec((tb, d), lambda i, h: (i, 0)), pl.BlockSpec((2 * half, d), lambda i, h: (h, 0)),
                  pl.BlockSpec((1, nk, half), lambda i, h: (h, 0, 0)),
                  pl.BlockSpec((1, nk, half), lambda i, h: (h, 0, 0))],
        out_specs=[pl.BlockSpec((1, PEER_TOPK, tb), lambda i, h: (h, 0, i)),
                   pl.BlockSpec((1, PEER_TOPK, tb), lambda i, h: (h, 0, i))],
        out_shape=[jax.ShapeDtypeStruct((hp, PEER_TOPK, t), I32), jax.ShapeDtypeStruct((hp, PEER_TOPK, t), F32)],
        compiler_params=_cp("parallel", "arbitrary"))(h2, wq_t, k1, k2)


def _unpack_pair(w):
    return lax.bitcast_convert_type(w << 16, F32), lax.bitcast_convert_type(w & jnp.int32(-65536), F32)


def _peer_gather_kernel(e_ref, h_ref, g_ref, x1_ref, gate_ref, tab_ref, y_ref, buf_ref, sem_ref):
    tb, d = h_ref.shape
    npick = g_ref.shape[0]
    half = d // 2

    def row_copy(e, slot, p):
        return pltpu.make_async_copy(tab_ref.at[pl.ds(e, 1), :], buf_ref.at[slot, pl.ds(p, 1), :], sem_ref.at[slot])

    def issue(j, slot):
        def body(p, c):
            row_copy(e_ref[j, p], slot, p).start()
            return c
        lax.fori_loop(0, npick, body, 0, unroll=8)

    def wait(slot):
        pltpu.make_async_copy(tab_ref.at[pl.ds(0, npick), :], buf_ref.at[slot], sem_ref.at[slot]).wait()

    def compute(j, slot):
        w = buf_ref[slot]
        x = h_ref[pl.ds(j, 1), :]
        u_lo, u_hi = _unpack_pair(w[:, :half])
        hid = jnp.sum(u_lo * x[:, :half] + u_hi * x[:, half:], axis=-1, keepdims=True)
        gw = g_ref.shape[1]
        lane = lax.broadcasted_iota(I32, (npick, gw), 1)
        jcol = (pl.program_id(0) * tb) % gw + j
        gcol = jnp.sum(jnp.where(lane == jcol, g_ref[...], 0.0), axis=-1, keepdims=True)
        wgt = gcol * jax.nn.gelu(hid)
        v_lo, v_hi = _unpack_pair(w[:, half:])
        o_lo = jnp.sum(wgt * v_lo, axis=0, keepdims=True)
        o_hi = jnp.sum(wgt * v_hi, axis=0, keepdims=True)
        gate = gate_ref[0] if gate_ref.shape[1] == 1 else gate_ref[0, pl.ds(j, 1), :]
        y_ref[pl.ds(j, 1), :half] = x1_ref[pl.ds(j, 1), :half] + gate[:, :half] * o_lo
        y_ref[pl.ds(j, 1), half:] = x1_ref[pl.ds(j, 1), half:] + gate[:, half:] * o_hi

    issue(0, 0)

    def body(j, c):
        slot = j % 2

        @pl.when(j + 1 < tb)
        def _():
            issue(j + 1, 1 - slot)

        wait(slot)
        compute(j, slot)
        return c

    lax.fori_loop(0, tb, body, 0)


def _peer_gather(e_tok, h2, g_t, x1, gate2, table, *, tb, rows_per_batch):
    t, d = h2.shape
    npick = g_t.shape[0]
    rowblk = pl.BlockSpec((tb, d), lambda i: (i, 0))
    gw = t if t < LANES else max(tb, LANES)
    assert gw % tb == 0 and t % gw == 0
    return pl.pallas_call(
        _peer_gather_kernel, grid=(t // tb,),
        in_specs=[pl.BlockSpec((tb, npick), lambda i: (i, 0), memory_space=pltpu.SMEM), rowblk,
                  pl.BlockSpec((npick, gw), lambda i: (0, (i * tb) // gw)), rowblk,
                  _mod_spec(gate2, tb, rows_per_batch),
                  pl.BlockSpec(memory_space=pl.ANY)],
        out_specs=rowblk, out_shape=jax.ShapeDtypeStruct((t, d), F32),
        scratch_shapes=[pltpu.VMEM((2, npick, d), I32), pltpu.SemaphoreType.DMA((2,))],
        compiler_params=_cp("arbitrary", disable_bounds_checks=True))(e_tok, h2, g_t, x1, gate2, table)


def _pack_bf16_pairs(t):
    half = t.shape[1] // 2
    b = lax.bitcast_convert_type(t.astype(BF16), jnp.uint16).astype(jnp.uint32)
    return lax.bitcast_convert_type(b[:, :half] | (b[:, half:] << 16), I32)


def _block_diag(m, ns):
    g, a, b = m.shape
    gs = g // ns
    eye = jnp.eye(gs, dtype=m.dtype)
    return jnp.einsum('sgab,gh->sgahb', m.reshape(ns, gs, a, b), eye).reshape(ns, gs * a, gs * b)


def _trunk(x, mods, wts, ssm, peer, h0_re, h0_im, attend, *, nb, seq, dims):
    t, d = x.shape
    shift1, scale1, gate1, shift2, scale2, gate2 = mods
    per_batch_mod = shift1.shape[1] == 1
    tm = min(256, seq) if per_batch_mod else t
    big_tm = min(512, seq) if per_batch_mod else t
    q, k, v = _row_call(functools.partial(_proj_qkv_kernel, hd=dims['hd']), big_tm, seq, x, (shift1, scale1),
                        (wts['norm1_g'], wts['w_q'], wts['w_k'], wts['w_v'], wts['q_gain'], wts['k_gain']),
                        ((dims['a'], BF16), (dims['kv'], F32), (dims['kv'], F32)))
    qi, kw, u = _row_call(_proj_idx_kernel, big_tm, seq, x, (shift1, scale1),
                          (wts['norm1_g'], wts['w_qi'], wts['w_kw'], wts['w_u']),
                          ((dims['ih'] * dims['idim'], BF16), (LANES, F32), (dims['w'], F32)))
    a_n = attend(q, qi, kw, k, v)
    w = dims['w']
    u_tb = u.reshape(nb, seq, w).transpose(1, 0, 2).reshape(t, w)
    tc = min(64, seq)
    s_tb, h_re, h_im = _s5(u_tb, h0_re, h0_im, *ssm, r=nb, tc=tc)
    s_n = s_tb.reshape(seq, nb, w).transpose(1, 0, 2).reshape(t, w)
    x1, h2 = _out_proj(x, gate1, shift2, scale2, a_n, s_n, wts['w_out_a'], wts['w_out_s'], wts['norm2_g'],
                       tm=tm, rows_per_batch=seq)
    e, g = _peer_route(h2, peer['wq_t'], peer['k1'], peer['k2'], tb=tm)
    npick = e.shape[0] * e.shape[1]
    tb = min(64, tm)
    y = _peer_gather(e.reshape(npick, t).T, h2, g.reshape(npick, t), x1, gate2, peer['table'],
                     tb=tb, rows_per_batch=seq)
    return y, k, v, kw[:, :dims['idim']], h_re, h_im


def kernel(x_prompt, x_sample, cache_k, cache_v, cache_kidx, state_ssm_re, state_ssm_im, page_table, c_prompt, c_sample, w_ada, b_ada, norm1_g, w_in, q_gain, k_gain, ssm_A_re, ssm_A_im, ssm_B_re, ssm_B_im, ssm_C_re, ssm_C_im, ssm_D, ssm_log_dt, ssm_w_glu, ssm_b_glu, attn_out_g, ssm_out_g, w_out, norm2_g, peer_w_q, peer_sub_keys1, peer_sub_keys2, peer_u, peer_v):
    b, seq, d = x_prompt.shape
    db, dseq, _ = x_sample.shape
    assert dseq == 1
    hd = q_gain.shape[0]
    nkv = cache_k.shape[2]
    idim = cache_kidx.shape[2]
    g, p, c = ssm_B_re.shape
    w = g * c
    a = d - w
    nh = a // hd
    kv = nkv * hd
    ih = (w_in.shape[1] - a - 2 * kv - idim - w) // (idim + 1)
    assert a + 2 * kv + ih * idim + idim + ih + w == w_in.shape[1] and idim + ih <= LANES
    dims = dict(hd=hd, a=a, kv=kv, ih=ih, idim=idim, w=w)

    c_all = jnp.concatenate([c_prompt, c_sample], axis=0)
    pad = (-c_all.shape[0]) % SUBLANES
    c_all = jnp.pad(c_all, ((0, pad), (0, 0)))
    m = _adaln(c_all, w_ada, b_ada)
    mods_p = [mm[:b].reshape(b, 1, d) for mm in jnp.split(m, 6, axis=-1)]
    mods_s = [mm[b:b + db].reshape(1, db, d) for mm in jnp.split(m, 6, axis=-1)]

    cuts = [a, a + kv, a + 2 * kv, a + 2 * kv + ih * idim, a + 2 * kv + ih * idim + idim + ih]
    w_q, w_k, w_v, w_qi, w_kw, w_u = [x.astype(BF16) for x in jnp.split(w_in, cuts, axis=1)]
    w_kw = jnp.pad(w_kw, ((0, 0), (0, LANES - idim - ih)))
    row = lambda x: x.reshape(1, -1)
    wts = dict(norm1_g=row(norm1_g), w_q=w_q, w_k=w_k, w_v=w_v, w_qi=w_qi, w_kw=w_kw, w_u=w_u,
               q_gain=row(q_gain), k_gain=row(k_gain), w_out_a=w_out[:a].astype(BF16),
               w_out_s=w_out[a:].astype(BF16), norm2_g=row(norm2_g))

    gp_shape = jax.ShapeDtypeStruct((g, p), F32)
    cgp_shape = jax.ShapeDtypeStruct((c, g, p), F32)
    ab_re, ab_im, bb_re, bb_im = pl.pallas_call(
        _ssm_param_kernel, out_shape=[gp_shape, gp_shape, cgp_shape, cgp_shape], compiler_params=_cp())(
            ssm_A_re, ssm_A_im, ssm_log_dt.reshape(g, 1), ssm_B_re.transpose(2, 0, 1), ssm_B_im.transpose(2, 0, 1))
    ns = max(1, g // SSM_SLAB_GROUPS)
    ssm = (ab_re.reshape(1, g * p), ab_im.reshape(1, g * p),
           _block_diag(bb_re.transpose(1, 0, 2), ns).astype(BF16), _block_diag(bb_im.transpose(1, 0, 2), ns).astype(BF16),
           _block_diag(ssm_C_re.transpose(0, 2, 1), ns).astype(BF16),
           _block_diag(ssm_C_im.transpose(0, 2, 1), ns).astype(BF16),
           ssm_D.reshape(1, w), ssm_w_glu.astype(BF16), row(ssm_b_glu), row(ssm_out_g))

    hp, nk, half = peer_sub_keys1.shape
    peer = dict(wq_t=peer_w_q.T.astype(BF16), k1=peer_sub_keys1.astype(BF16), k2=peer_sub_keys2.astype(BF16),
                table=jnp.concatenate([_pack_bf16_pairs(peer_u), _pack_bf16_pairs(peer_v)], axis=1))
    og = row(attn_out_g)

    attend_p = functools.partial(_prompt_attention, og=og, b=b, nh=nh, nkv=nkv, hd=hd, ih=ih, idim=idim)
    attend_s = functools.partial(_sample_attention, og=og, cache_k=cache_k, cache_v=cache_v, cache_kidx=cache_kidx,
                                 page_table=page_table, nh=nh, nkv=nkv, hd=hd, ih=ih, idim=idim)
    zeros = jnp.zeros((b, g * p), F32)
    yp, k_p, v_p, ki_p, re_p, im_p = _trunk(x_prompt.reshape(b * seq, d), mods_p, wts, ssm, peer, zeros, zeros,
                                            attend_p, nb=b, seq=seq, dims=dims)
    ys, k_s, v_s, ki_s, re_s, im_s = _trunk(x_sample.reshape(db, d), mods_s, wts, ssm, peer,
                                            state_ssm_re.reshape(db, g * p), state_ssm_im.reshape(db, g * p),
                                            attend_s, nb=db, seq=1, dims=dims)
    return (yp.reshape(b, seq, d), ys.reshape(db, 1, d),
            k_p.reshape(b, seq, nkv, hd), v_p.reshape(b, seq, nkv, hd), ki_p.reshape(b, seq, idim),
            re_p.reshape(b, g, p), im_p.reshape(b, g, p),
            k_s.reshape(db, 1, nkv, hd), v_s.reshape(db, 1, nkv, hd), ki_s.reshape(db, 1, idim),
            re_s.reshape(db, g, p), im_s.reshape(db, g, p))
```

```python
import functools

import jax
import jax.numpy as jnp
from jax import lax
from jax.experimental import pallas as pl
from jax.experimental.pallas import tpu as pltpu

F32 = jnp.float32
BF16 = jnp.bfloat16
I32 = jnp.int32

RMS_EPS = 1e-6
TOPK_MAX = 256
PEER_TOPK = 16
LANES = 128
SUBLANES = 8
SSM_SLAB_GROUPS = 16
INT_MIN = -(2 ** 31)
VMEM_LIMIT = 56 * 1024 * 1024


def _cp(*sem, vmem=VMEM_LIMIT, **kw):
    return pltpu.CompilerParams(dimension_semantics=sem, vmem_limit_bytes=vmem, **kw)


def _nt_dot(a, b):
    return lax.dot_general(a, b, (((1,), (1,)), ((), ())), preferred_element_type=F32)


def _dot(a, b):
    return jnp.dot(a, b, preferred_element_type=F32)


def _rms(x, g):
    return x * lax.rsqrt(jnp.mean(x * x, axis=-1, keepdims=True) + RMS_EPS) * g


def _mod_spec(mod, tm, rows_per_batch):
    _, r, d = mod.shape
    if r == 1:
        return pl.BlockSpec((1, 1, d), lambda i, *_: ((i * tm) // rows_per_batch, 0, 0))
    assert r == tm
    return pl.BlockSpec((1, r, d), lambda i, *_: (0, 0, 0))


def _adaln_kernel(c_ref, w_ref, b_ref, o_ref):
    c = c_ref[...]
    s = (c * jax.nn.sigmoid(c)).astype(BF16)
    o_ref[...] = _dot(s, w_ref[...].astype(BF16)) + b_ref[...]


def _adaln(c, w, b):
    m, d = c.shape
    n = w.shape[1]
    tn = 512
    return pl.pallas_call(
        _adaln_kernel, grid=(n // tn,),
        in_specs=[pl.BlockSpec((m, d), lambda j: (0, 0)), pl.BlockSpec((d, tn), lambda j: (0, j)),
                  pl.BlockSpec((1, tn), lambda j: (0, j))],
        out_specs=pl.BlockSpec((m, tn), lambda j: (0, j)),
        out_shape=jax.ShapeDtypeStruct((m, n), F32), compiler_params=_cp("parallel"))(c, w, b.reshape(1, n))


def _head_rms_store(o_ref, p, gain, hd):
    for h in range(p.shape[1] // hd):
        o_ref[:, h * hd:(h + 1) * hd] = _rms(p[:, h * hd:(h + 1) * hd], gain).astype(o_ref.dtype)


def _proj_qkv_kernel(x_ref, sh_ref, sc_ref, g_ref, wq_ref, wk_ref, wv_ref, qg_ref, kg_ref,
                     q_ref, k_ref, v_ref, *, hd):
    hb = (_rms(x_ref[...], g_ref[...]) * (1.0 + sc_ref[0]) + sh_ref[0]).astype(BF16)
    _head_rms_store(q_ref, _dot(hb, wq_ref[...]), qg_ref[...], hd)
    _head_rms_store(k_ref, _dot(hb, wk_ref[...]), kg_ref[...], hd)
    v_ref[...] = _dot(hb, wv_ref[...])


def _proj_idx_kernel(x_ref, sh_ref, sc_ref, g_ref, wqi_ref, wkw_ref, wu_ref, qi_ref, kw_ref, u_ref):
    hb = (_rms(x_ref[...], g_ref[...]) * (1.0 + sc_ref[0]) + sh_ref[0]).astype(BF16)
    qi_ref[...] = _dot(hb, wqi_ref[...]).astype(qi_ref.dtype)
    kw_ref[...] = _dot(hb, wkw_ref[...])
    u_ref[...] = _dot(hb, wu_ref[...])


def _row_call(kernel, tm, rows_per_batch, x, mods, consts, outs):
    t, d = x.shape
    full = lambda a: pl.BlockSpec(a.shape, lambda i: (0,) * a.ndim)
    return pl.pallas_call(
        kernel, grid=(t // tm,),
        in_specs=[pl.BlockSpec((tm, d), lambda i: (i, 0))] + [_mod_spec(m, tm, rows_per_batch) for m in mods]
        + [full(c) for c in consts],
        out_specs=[pl.BlockSpec((tm, n), lambda i: (i, 0)) for n, _ in outs],
        out_shape=[jax.ShapeDtypeStruct((t, n), dt) for n, dt in outs],
        compiler_params=_cp("parallel"))(x, *mods, *consts)


def _order_key(s):
    bits = lax.bitcast_convert_type(s, I32)
    bits = jnp.where(s == 0.0, 0, bits)
    return jnp.where(bits < 0, bits ^ 0x7FFFFFFF, bits)


def _topk_mask(key, col, k, extra_key=None, ncols=None):
    m = key.shape[0]
    kf = float(k)

    def count_ge(c):
        n = jnp.sum((key >= c).astype(F32), axis=-1, keepdims=True)
        if extra_key is not None:
            n = n + (extra_key >= c).astype(F32)
        return n

    def bit_step(it, v):
        cand = v + lax.shift_left(jnp.int32(1), 31 - it)
        return jnp.where(count_ge(cand) >= kf, cand, v)

    v = lax.fori_loop(0, 32, bit_step, jnp.full((m, 1), INT_MIN, I32))
    gt = key > v
    eq = key == v
    n_gt = jnp.sum(gt.astype(F32), axis=-1, keepdims=True)
    if extra_key is not None:
        n_gt = n_gt + (extra_key > v).astype(F32)
    need = kf - n_gt
    nbits = max(1, (ncols - 1).bit_length())

    def col_step(it, lim):
        cand = lim + lax.shift_left(jnp.int32(1), nbits - 1 - it)
        n = jnp.sum((eq & (col < cand)).astype(F32), axis=-1, keepdims=True)
        return jnp.where(n < need, cand, lim)

    lim = lax.fori_loop(0, nbits, col_step, jnp.zeros((m, 1), I32))
    mask = gt | (eq & (col <= lim))
    if extra_key is None:
        return mask, None
    n_eq = jnp.sum(eq.astype(F32), axis=-1, keepdims=True)
    extra_sel = (extra_key > v) | ((extra_key == v) & (n_eq < need))
    return mask, extra_sel


def _prompt_attn_kernel(q_ref, qi_ref, kwq_ref, k_ref, v_ref, kwall_ref, og_ref, o_ref, acc_ref,
                        *, nh, nkv, hd, ih, idim, topk, q0):
    tq = q_ref.shape[0]
    s_len = k_ref.shape[1]
    i = pl.program_id(1) + q0
    kib = kwall_ref[0, :, :idim].astype(BF16)
    w = kwq_ref[:, idim:idim + ih]
    qi = qi_ref[...]
    sc = jnp.zeros((tq, s_len), F32)
    for h in range(ih):
        r = _nt_dot(qi[:, h * idim:(h + 1) * idim], kib)
        sc = sc + jnp.maximum(r, 0.0) * w[:, h:h + 1]
    col = lax.broadcasted_iota(I32, (tq, s_len), 1)
    row = lax.broadcasted_iota(I32, (tq, s_len), 0) + i * tq
    adm = col <= row
    key = jnp.where(adm, _order_key(sc), INT_MIN)
    sel, _ = _topk_mask(key, col, topk, ncols=s_len)
    sel = sel & adm
    kb = k_ref[0].astype(BF16)
    vb = v_ref[0].astype(BF16)
    g = nh // nkv
    scale = hd ** -0.5
    for hk in range(nkv):
        kh = kb[:, hk * hd:(hk + 1) * hd]
        vh = vb[:, hk * hd:(hk + 1) * hd]
        for j in range(g):
            h = hk * g + j
            s = _nt_dot(q_ref[:, h * hd:(h + 1) * hd], kh) * scale
            s = jnp.where(sel, s, -jnp.inf)
            p = jnp.exp(s - jnp.max(s, axis=-1, keepdims=True))
            l = jnp.sum(p, axis=-1, keepdims=True)
            acc_ref[:, h * hd:(h + 1) * hd] = _dot(p.astype(BF16), vh) / l
    o_ref[0] = _rms(acc_ref[...], og_ref[...]).astype(o_ref.dtype)


PROMPT_KEY_CLASSES = 8


def _prompt_attention(q, qi, kw, k, v, og, *, b, nh, nkv, hd, ih, idim):
    t, a = q.shape
    s_len = t // b
    tq = min(128, s_len)
    nq = s_len // tq
    topk = min(TOPK_MAX, s_len // 4)
    ncls = min(PROMPT_KEY_CLASSES, nq)
    per = nq // ncls
    assert per * ncls == nq
    k3, v3, kw3 = (x.reshape(b, s_len, x.shape[1]) for x in (k, v, kw))
    outs = []
    for c in range(ncls):
        q0 = c * per
        sk = (c + 1) * per * tq
        kern = functools.partial(_prompt_attn_kernel, nh=nh, nkv=nkv, hd=hd, ih=ih, idim=idim, topk=topk, q0=q0)
        blk = lambda n, q0=q0: pl.BlockSpec((tq, n), lambda bi, i: (bi * nq + q0 + i, 0))
        keys = lambda n, sk=sk: pl.BlockSpec((1, sk, n), lambda bi, i: (bi, 0, 0))
        outs.append(pl.pallas_call(
            kern, grid=(b, per),
            in_specs=[blk(a), blk(qi.shape[1]), blk(kw.shape[1]), keys(k.shape[1]), keys(v.shape[1]),
                      keys(kw.shape[1]), pl.BlockSpec((1, a), lambda bi, i: (0, 0))],
            out_specs=pl.BlockSpec((1, tq, a), lambda bi, i: (bi, i, 0)),
            out_shape=jax.ShapeDtypeStruct((b, per * tq, a), BF16),
            scratch_shapes=[pltpu.VMEM((tq, a), F32)],
            compiler_params=_cp("parallel", "parallel"))(q, qi, kw, k3, v3, kw3, og))
    return jnp.concatenate(outs, axis=1).reshape(t, a)


def _sample_score_kernel(pt_ref, qi_ref, w_ref, kn_ref, kc_ref, o_ref, sn_ref, buf_ref, sem_ref):
    db, npg = pt_ref.shape
    idim, page = kc_ref.shape[1], kc_ref.shape[2]

    def page_copy(page_id, slot, pg):
        return pltpu.make_async_copy(kc_ref.at[page_id],
                                     buf_ref.at[slot, :, pl.ds(pl.multiple_of(pg * page, page), page)],
                                     sem_ref.at[slot])

    def issue(b, slot):
        def body(pg, c):
            page_copy(pt_ref[b, pg], slot, pg).start()
            return c
        lax.fori_loop(0, npg, body, 0, unroll=8)

    def wait(slot):
        def body(pg, c):
            page_copy(0, slot, pg).wait()
            return c
        lax.fori_loop(0, npg, body, 0, unroll=8)

    issue(0, 0)

    def row(b, c):
        slot = b % 2

        @pl.when(b + 1 < db)
        def _():
            issue(b + 1, 1 - slot)

        wait(slot)
        qi = qi_ref[b]
        w = w_ref[b]
        r = _dot(qi, buf_ref[slot].astype(BF16))
        o_ref[pl.ds(b, 1), :] = jnp.sum(jnp.maximum(r, 0.0) * w, axis=0, keepdims=True)
        rn = _nt_dot(qi, jnp.broadcast_to(kn_ref[b], (LANES, idim)).astype(BF16))
        sn_ref[pl.ds(b, 1), :] = jnp.sum(jnp.maximum(rn, 0.0) * w, axis=0, keepdims=True)
        return c

    lax.fori_loop(0, db, row, 0)


def _sample_select_kernel(sc_ref, sn_ref, m_ref, ns_ref, *, topk):
    sc = sc_ref[...]
    db, past = sc.shape
    col = lax.broadcasted_iota(I32, (db, past), 1)
    mask, new_sel = _topk_mask(_order_key(sc), col, topk, extra_key=_order_key(sn_ref[:, :1]), ncols=past)
    m_ref[...] = mask.astype(F32)
    ns_ref[...] = jnp.broadcast_to(new_sel.astype(F32), ns_ref.shape)


def _sample_attn_kernel(pt_ref, q_ref, m_ref, ns_ref, kn_ref, vn_ref, kc_ref, vc_ref, og_ref, o_ref,
                        kbuf_ref, vbuf_ref, sem_ref, mx_ref, l_ref, acc_ref, *, nh, nkv, hd):
    db, npg = pt_ref.shape
    cp = kbuf_ref.shape[1]
    page = kbuf_ref.shape[2] // nkv
    nchunk = npg // cp
    total = db * nchunk
    g = nh // nkv
    scale = hd ** -0.5

    def page_copies(page_id, slot, pg):
        return (pltpu.make_async_copy(kc_ref.at[page_id], kbuf_ref.at[slot, pg], sem_ref.at[0, slot]),
                pltpu.make_async_copy(vc_ref.at[page_id], vbuf_ref.at[slot, pg], sem_ref.at[1, slot]))

    def issue(gi, slot):
        b = gi // nchunk
        c = gi % nchunk
        for pg in range(cp):
            for cpy in page_copies(pt_ref[b, c * cp + pg], slot, pg):
                cpy.start()

    def wait(slot):
        for pg in range(cp):
            for cpy in page_copies(0, slot, pg):
                cpy.wait()

    def per_kv_head(fn, n):
        head_kv = lax.broadcasted_iota(I32, (nh, n), 0) // g
        out = fn(0)
        for hk in range(1, nkv):
            out = jnp.where(head_kv == hk, fn(hk), out)
        return out

    def update(qb, key_of, val_of, n, keep):
        s = per_kv_head(lambda hk: _nt_dot(qb, key_of(hk)), n) * scale
        s = jnp.where(keep, s, -jnp.inf)
        m_old = mx_ref[...]
        m_new = jnp.maximum(m_old, jnp.max(s, axis=-1, keepdims=True))
        m_safe = jnp.where(m_new == -jnp.inf, 0.0, m_new)
        alpha = jnp.exp(m_old - m_safe)
        p = jnp.exp(s - m_safe)
        pb = p.astype(BF16)
        l_ref[...] = alpha * l_ref[...] + jnp.sum(p, axis=-1, keepdims=True)
        acc_ref[...] = alpha * acc_ref[...] + per_kv_head(lambda hk: _dot(pb, val_of(hk)), hd)
        mx_ref[...] = m_new

    issue(0, 0)

    def step(gi, carry):
        slot = gi % 2
        b = gi // nchunk
        c = gi % nchunk

        @pl.when(gi + 1 < total)
        def _():
            issue(gi + 1, 1 - slot)

        wait(slot)

        @pl.when(c == 0)
        def _():
            mx_ref[...] = jnp.full(mx_ref.shape, -jnp.inf, F32)
            l_ref[...] = jnp.zeros(l_ref.shape, F32)
            acc_ref[...] = jnp.zeros(acc_ref.shape, F32)

        qb = q_ref[b].astype(BF16)
        head_rows = lambda buf, hk: buf[slot, :, pl.ds(hk, page, stride=nkv), :].reshape(cp * page, hd).astype(BF16)
        update(qb, lambda hk: head_rows(kbuf_ref, hk), lambda hk: head_rows(vbuf_ref, hk), cp * page,
               m_ref[b, c] > 0.0)

        @pl.when(c == nchunk - 1)
        def _():
            n8 = kn_ref.shape[1]
            first = lax.broadcasted_iota(I32, (1, n8), 1) == 0
            kn = kn_ref[b].astype(BF16)
            vn = vn_ref[b].astype(BF16)
            update(qb, lambda hk: kn[:, hk * hd:(hk + 1) * hd], lambda hk: vn[:, hk * hd:(hk + 1) * hd], n8,
                   first & (ns_ref[pl.ds(b, 1), :1] > 0.0))
            o = acc_ref[...] / l_ref[...]
            ms = jnp.sum(jnp.sum(o * o, axis=-1, keepdims=True), axis=0, keepdims=True) / (nh * hd)
            o_ref[b] = o * lax.rsqrt(ms + RMS_EPS) * og_ref[...]

        return carry

    lax.fori_loop(0, total, step, 0)


SAMPLE_PAGES_PER_CHUNK = 16


def _sample_attention(q, qi, kw, k, v, og, cache_k, cache_v, cache_kidx, page_table, *, nh, nkv, hd, ih, idim):
    db = q.shape[0]
    npg = page_table.shape[1]
    page = cache_k.shape[1]
    past = npg * page
    topk = min(TOPK_MAX, (past + 1) // 4)
    qi3 = qi.reshape(db, ih, idim)
    w3 = kw[:, idim:idim + ih].reshape(db, ih, 1)
    vmem = pl.BlockSpec(memory_space=pltpu.VMEM)
    smem = pl.BlockSpec(memory_space=pltpu.SMEM)
    hbm = pl.BlockSpec(memory_space=pl.ANY)
    scores, s_new = pl.pallas_call(
        _sample_score_kernel,
        in_specs=[smem, vmem, vmem, vmem, hbm], out_specs=[vmem, vmem],
        out_shape=[jax.ShapeDtypeStruct((db, past), F32), jax.ShapeDtypeStruct((db, LANES), F32)],
        scratch_shapes=[pltpu.VMEM((2, idim, past), F32), pltpu.SemaphoreType.DMA((2,))],
        compiler_params=_cp())(page_table, qi3, w3, kw[:, :idim].reshape(db, 1, idim),
                               cache_kidx.transpose(0, 2, 1))
    mask, new_sel = pl.pallas_call(
        functools.partial(_sample_select_kernel, topk=topk),
        out_shape=[jax.ShapeDtypeStruct((db, past), F32), jax.ShapeDtypeStruct((db, LANES), F32)],
        compiler_params=_cp())(scores, s_new)
    cp = min(SAMPLE_PAGES_PER_CHUNK, npg)
    assert npg % cp == 0
    rep = lambda a: jnp.broadcast_to(a[:, None, :], (db, SUBLANES, a.shape[1]))
    out = pl.pallas_call(
        functools.partial(_sample_attn_kernel, nh=nh, nkv=nkv, hd=hd),
        in_specs=[smem, vmem, vmem, vmem, vmem, vmem, hbm, hbm, vmem], out_specs=vmem,
        out_shape=jax.ShapeDtypeStruct((db, nh, hd), F32),
        scratch_shapes=[pltpu.VMEM((2, cp, page * nkv, hd), F32), pltpu.VMEM((2, cp, page * nkv, hd), F32),
                        pltpu.SemaphoreType.DMA((2, 2)),
                        pltpu.VMEM((nh, 1), F32), pltpu.VMEM((nh, 1), F32), pltpu.VMEM((nh, hd), F32)],
        compiler_params=_cp())(
            page_table, q.astype(F32).reshape(db, nh, hd), mask.reshape(db, npg // cp, 1, cp * page), new_sel,
            rep(k), rep(v), cache_k.reshape(-1, page * nkv, hd), cache_v.reshape(-1, page * nkv, hd),
            og.reshape(nh, hd))
    return out.reshape(db, nh * hd).astype(BF16)


def _ssm_param_kernel(are_ref, aim_ref, ldt_ref, bre_ref, bim_ref, abre_ref, abim_ref, bbre_ref, bbim_ref):
    a_re = are_ref[...]
    a_im = aim_ref[...]
    dt = jnp.exp(ldt_ref[...])
    mag = jnp.exp(a_re * dt)
    ab_re = mag * jnp.cos(a_im * dt)
    ab_im = mag * jnp.sin(a_im * dt)
    den = a_re * a_re + a_im * a_im
    nr = ab_re - 1.0
    co_re = (nr * a_re + ab_im * a_im) / den
    co_im = (ab_im * a_re - nr * a_im) / den
    abre_ref[...] = ab_re
    abim_ref[...] = ab_im
    b_re = bre_ref[...]
    b_im = bim_ref[...]
    bbre_ref[...] = co_re[None] * b_re - co_im[None] * b_im
    bbim_ref[...] = co_re[None] * b_im + co_im[None] * b_re


def _s5_kernel(u_ref, h0re_ref, h0im_ref, ar_ref, ai_ref, wre_ref, wim_ref, cre_ref, cim_ref, d_ref,
               wglu_ref, bglu_ref, og_ref, s_ref, hre_ref, him_ref, bure_ref, buim_ref, y_ref, *, r, tc):
    ci = pl.program_id(0)
    ns, sc_w, sp_w = wre_ref.shape

    @pl.when(ci == 0)
    def _():
        hre_ref[...] = h0re_ref[...]
        him_ref[...] = h0im_ref[...]

    for s in range(ns):
        cols = slice(s * sc_w, (s + 1) * sc_w)
        st = slice(s * sp_w, (s + 1) * sp_w)
        ub = u_ref[:, cols].astype(BF16)
        bure_ref[...] = _dot(ub, wre_ref[s])
        buim_ref[...] = _dot(ub, wim_ref[s])
        a_r = jnp.broadcast_to(ar_ref[:, st], (r, sp_w))
        a_i = jnp.broadcast_to(ai_ref[:, st], (r, sp_w))

        def step(t, carry):
            hr, hi = carry
            rows = pl.ds(pl.multiple_of(t * r, r), r)
            nr = a_r * hr - a_i * hi + bure_ref[rows, :]
            ni = a_r * hi + a_i * hr + buim_ref[rows, :]
            bure_ref[rows, :] = nr
            buim_ref[rows, :] = ni
            return nr, ni

        hr, hi = lax.fori_loop(0, tc, step, (hre_ref[:, st], him_ref[:, st]))
        hre_ref[:, st] = hr
        him_ref[:, st] = hi
        y_ref[:, cols] = (_dot(bure_ref[...].astype(BF16), cre_ref[s])
                          - _dot(buim_ref[...].astype(BF16), cim_ref[s]))
    y = jax.nn.gelu(y_ref[...] + d_ref[...] * u_ref[...])
    z = _dot(y.astype(BF16), wglu_ref[...]) + bglu_ref[...]
    s_ref[...] = _rms(y * jax.nn.sigmoid(z), og_ref[...]).astype(s_ref.dtype)


def _s5(u_tb, h0_re, h0_im, ab_re, ab_im, w_re, w_im, c_re, c_im, d, w_glu, b_glu, og, *, r, tc):
    rows, w = u_tb.shape
    gp = h0_re.shape[1]
    full = lambda a: pl.BlockSpec(a.shape, lambda i: (0,) * a.ndim)
    consts = (h0_re, h0_im, ab_re, ab_im, w_re, w_im, c_re, c_im, d, w_glu, b_glu, og)
    sp_w = w_re.shape[2]
    return pl.pallas_call(
        functools.partial(_s5_kernel, r=r, tc=tc), grid=(rows // (r * tc),),
        in_specs=[pl.BlockSpec((r * tc, w), lambda i: (i, 0))] + [full(c) for c in consts],
        out_specs=[pl.BlockSpec((r * tc, w), lambda i: (i, 0)), full(h0_re), full(h0_im)],
        out_shape=[jax.ShapeDtypeStruct((rows, w), BF16), jax.ShapeDtypeStruct((r, gp), F32),
                   jax.ShapeDtypeStruct((r, gp), F32)],
        scratch_shapes=[pltpu.VMEM((r * tc, sp_w), F32), pltpu.VMEM((r * tc, sp_w), F32),
                        pltpu.VMEM((r * tc, w), F32)],
        compiler_params=_cp("arbitrary"))(u_tb, *consts)


def _out_proj_kernel(x_ref, g1_ref, sh_ref, sc_ref, a_ref, s_ref, wa_ref, ws_ref, n2_ref, x1_ref, h2_ref):
    mix = _dot(a_ref[...], wa_ref[...]) + _dot(s_ref[...], ws_ref[...])
    x1 = x_ref[...] + g1_ref[0] * mix
    x1_ref[...] = x1
    h2_ref[...] = _rms(x1, n2_ref[...]) * (1.0 + sc_ref[0]) + sh_ref[0]


def _out_proj(x, gate1, shift2, scale2, a_n, s_n, w_a, w_s, n2, *, tm, rows_per_batch):
    t, d = x.shape
    full = lambda a: pl.BlockSpec(a.shape, lambda i: (0,) * a.ndim)
    rowblk = lambda n: pl.BlockSpec((tm, n), lambda i: (i, 0))
    return pl.pallas_call(
        _out_proj_kernel, grid=(t // tm,),
        in_specs=[rowblk(d)] + [_mod_spec(m, tm, rows_per_batch) for m in (gate1, shift2, scale2)]
        + [rowblk(a_n.shape[1]), rowblk(s_n.shape[1]), full(w_a), full(w_s), full(n2)],
        out_specs=[rowblk(d), rowblk(d)],
        out_shape=[jax.ShapeDtypeStruct((t, d), F32), jax.ShapeDtypeStruct((t, d), F32)],
        compiler_params=_cp("parallel"))(x, gate1, shift2, scale2, a_n, s_n, w_a, w_s, n2)


def _take_top(s, pay, k):
    n, tb = s.shape
    rid = lax.broadcasted_iota(I32, (n, tb), 0).astype(F32)
    kid = lax.broadcasted_iota(I32, (k, tb), 0)
    vals = jnp.zeros((k, tb), F32)
    pays = jnp.zeros((k, tb), F32)
    for it in range(k):
        m = jnp.max(s, axis=0, keepdims=True)
        idx = jnp.min(jnp.where(s == m, rid, float(n)), axis=0, keepdims=True)
        hit = rid == idx
        pv = jnp.max(jnp.where(hit, pay, -1.0), axis=0, keepdims=True)
        vals = jnp.where(kid == it, m, vals)
        pays = jnp.where(kid == it, pv, pays)
        s = jnp.where(hit, -jnp.inf, s)
    return vals, pays


def _peer_route_kernel(h_ref, wq_ref, k1_ref, k2_ref, e_ref, g_ref, *, nk):
    half = k1_ref.shape[2]
    tb = h_ref.shape[0]
    k = PEER_TOPK
    qt = _nt_dot(wq_ref[...], h_ref[...].astype(BF16))
    s1 = _dot(k1_ref[0], qt[:half, :].astype(BF16))
    s2 = _dot(k2_ref[0], qt[half:, :].astype(BF16))
    rid = lax.broadcasted_iota(I32, (nk, tb), 0).astype(F32)
    v1, i1 = _take_top(s1, rid, k)
    v2, i2 = _take_top(s2, rid, k)
    cand = jnp.concatenate([v1[a:a + 1, :] + v2 for a in range(k)], axis=0)
    ids = jnp.concatenate([i1[a:a + 1, :] * float(nk) + i2 for a in range(k)], axis=0)
    sc, e = _take_top(cand, ids, k)
    p = jnp.exp(sc - jnp.max(sc, axis=0, keepdims=True))
    g_ref[0] = p / jnp.sum(p, axis=0, keepdims=True)
    e_ref[0] = e.astype(I32)


def _peer_route(h2, wq_t, k1, k2, *, tb):
    t, d = h2.shape
    hp, nk, half = k1.shape
    return pl.pallas_call(
        functools.partial(_peer_route_kernel, nk=nk), grid=(t // tb, hp),
        in_specs=[pl.BlockSpec((tb, d), lambda i, h: (i, 0)), pl.BlockSpec((2 * half, d), lambda i, h: (h, 0)),
                  pl.BlockSpec((1, nk, half), lambda i, h: (h, 0, 0)),
                  pl.BlockSpec((1, nk, half), lambda i, h: (h, 0, 0))],
        out_specs=[pl.BlockSpec((1, PEER_TOPK, tb), lambda i, h: (h, 0, i)),
                   pl.BlockSpec((1, PEER_TOPK, tb), lambda i, h: (h, 0, i))],
        out_shape=[jax.ShapeDtypeStruct((hp, PEER_TOPK, t), I32), jax.ShapeDtypeStruct((hp, PEER_TOPK, t), F32)],
        compiler_params=_cp("parallel", "arbitrary"))(h2, wq_t, k1, k2)


def _unpack_pair(w):
    return lax.bitcast_convert_type(w << 16, F32), lax.bitcast_convert_type(w & jnp.int32(-65536), F32)


def _split_bf16(x):
    hi = x.astype(BF16).astype(F32)
    return hi, (x - hi).astype(BF16).astype(F32)


PEER_GATHER_BATCH = 2


def _peer_gather_kernel(e_ref, en_ref, h_ref, g_ref, x1_ref, gate_ref, tab_ref, y_ref, *scratch):
    *bufs, wgt_ref, sem_ref = scratch
    nbt = len(bufs) // 2
    tb = h_ref.shape[0]
    npick = bufs[0].shape[0]
    S = bufs[0].shape[1] // 2
    i = pl.program_id(0)

    def slab_copy(e, slot, p):
        return pltpu.make_async_copy(tab_ref.at[e], bufs[slot].at[p], sem_ref.at[slot])

    def issue(idx_ref, j, slot):
        for p in range(npick):
            slab_copy(idx_ref[j, p], slot, p).start()

    def wait(slot):
        pltpu.make_async_copy(tab_ref.at[pl.ds(0, npick)], bufs[slot], sem_ref.at[slot]).wait()

    group_sum = (lax.broadcasted_iota(I32, (npick, npick * 2 * S), 1) // (2 * S)
                 == lax.broadcasted_iota(I32, (npick, npick * 2 * S), 0)).astype(BF16)
    gw = g_ref.shape[1]
    g_lane = lax.broadcasted_iota(I32, (npick, gw), 1)

    def compute(j, slot):
        buf = bufs[slot]
        xb = pltpu.bitcast(h_ref[j], BF16)
        prods = [pltpu.bitcast(buf[p, :S, :], BF16) * xb for p in range(npick)]
        hid = jnp.sum(_dot(group_sum, jnp.concatenate(prods, axis=0)), axis=-1, keepdims=True)
        jcol = (i * tb) % gw + j
        gcol = jnp.sum(jnp.where(g_lane == jcol, g_ref[...], 0.0), axis=-1, keepdims=True)
        wgt_ref[slot % nbt] = jnp.broadcast_to(gcol * jax.nn.gelu(hid), (npick, LANES))
        nacc = 4
        acc_lo = [jnp.zeros((S, LANES), F32) for _ in range(nacc)]
        acc_hi = [jnp.zeros((S, LANES), F32) for _ in range(nacc)]
        for p in range(npick):
            v_lo, v_hi = _unpack_pair(buf[p, S:, :])
            w_p = jnp.broadcast_to(wgt_ref[slot % nbt, p:p + 1, :], (S, LANES))
            acc_lo[p % nacc] = acc_lo[p % nacc] + w_p * v_lo
            acc_hi[p % nacc] = acc_hi[p % nacc] + w_p * v_hi
        out = jnp.concatenate([sum(acc_lo[1:], acc_lo[0]), sum(acc_hi[1:], acc_hi[0])], axis=0)
        gate = gate_ref[0, 0] if gate_ref.shape[1] == 1 else gate_ref[0, j]
        y_ref[j] = x1_ref[j] + gate * out

    @pl.when(i == 0)
    def _():
        for s in range(nbt):
            issue(e_ref, s, s)

    def ring_turn(k, c):
        for side in range(2):
            j0 = (2 * k + side) * nbt
            for s in range(nbt):
                wait(side * nbt + s)
            for s in range(nbt):
                issue(en_ref, j0 + s, (1 - side) * nbt + s)
            for s in range(nbt):
                compute(j0 + s, side * nbt + s)
        return c

    lax.fori_loop(0, tb // (2 * nbt), ring_turn, 0)

    @pl.when(i == pl.num_programs(0) - 1)
    def _():
        for s in range(nbt):
            wait(s)


def _peer_gather(e_tok, h2, g_t, x1, gate2, table, *, tb, rows_per_batch):
    t, d = h2.shape
    npick = e_tok.shape[1]
    rows = table.shape[1]
    assert rows * LANES == d
    nbt = PEER_GATHER_BATCH
    assert tb % (2 * nbt) == 0 and t % tb == 0
    slab = lambda a: a.reshape(*a.shape[:-1], rows, LANES)
    gate = slab(gate2)
    if gate.shape[1] == 1:
        gate_spec = pl.BlockSpec((1, 1, rows, LANES), lambda i: ((i * tb) // rows_per_batch, 0, 0, 0))
    else:
        assert gate.shape[1] == tb
        gate_spec = pl.BlockSpec((1, tb, rows, LANES), lambda i: (0, 0, 0, 0))
    tokblk = pl.BlockSpec((tb, rows, LANES), lambda i: (i, 0, 0))
    idxblk = pl.BlockSpec((tb, npick), lambda i: (i, 0), memory_space=pltpu.SMEM)
    gw = t if t < LANES else max(tb, LANES)
    assert gw % tb == 0 and t % gw == 0
    y = pl.pallas_call(
        _peer_gather_kernel, grid=(t // tb,),
        in_specs=[idxblk, idxblk, pl.BlockSpec((tb, rows // 2, LANES), lambda i: (i, 0, 0)),
                  pl.BlockSpec((npick, gw), lambda i: (0, (i * tb) // gw)), tokblk,
                  gate_spec, pl.BlockSpec(memory_space=pl.ANY)],
        out_specs=tokblk, out_shape=jax.ShapeDtypeStruct((t, rows, LANES), F32),
        scratch_shapes=[pltpu.VMEM((npick, rows, LANES), I32)] * (2 * nbt)
        + [pltpu.VMEM((nbt, npick, LANES), F32), pltpu.SemaphoreType.DMA((2 * nbt,))],
        compiler_params=_cp("arbitrary", disable_bounds_checks=True))(
            e_tok, jnp.roll(e_tok, -nbt, axis=0), _pack_bf16_pairs(h2).reshape(t, rows // 2, LANES), g_t, slab(x1),
            gate, table)
    return y.reshape(t, d)


def _pack_bf16_pairs(t):
    half = t.shape[1] // 2
    b = lax.bitcast_convert_type(t.astype(BF16), jnp.uint16).astype(jnp.uint32)
    return lax.bitcast_convert_type(b[:, :half] | (b[:, half:] << 16), I32)


def _expert_slabs(u, v):
    e, d = u.shape
    s = d // 2 // LANES
    assert s * 2 * LANES == d and s % SUBLANES == 0
    return jnp.concatenate([_pack_bf16_pairs(u).reshape(e, s, LANES), _pack_bf16_pairs(v).reshape(e, s, LANES)],
                           axis=1)


def _block_diag(m, ns):
    g, a, b = m.shape
    gs = g // ns
    eye = jnp.eye(gs, dtype=m.dtype)
    return jnp.einsum('sgab,gh->sgahb', m.reshape(ns, gs, a, b), eye).reshape(ns, gs * a, gs * b)


def _trunk(x, mods, wts, ssm, peer, h0_re, h0_im, attend, *, nb, seq, dims):
    t, d = x.shape
    shift1, scale1, gate1, shift2, scale2, gate2 = mods
    per_batch_mod = shift1.shape[1] == 1
    tm = min(256, seq) if per_batch_mod else t
    big_tm = min(512, seq) if per_batch_mod else t
    q, k, v = _row_call(functools.partial(_proj_qkv_kernel, hd=dims['hd']), big_tm, seq, x, (shift1, scale1),
                        (wts['norm1_g'], wts['w_q'], wts['w_k'], wts['w_v'], wts['q_gain'], wts['k_gain']),
                        ((dims['a'], BF16), (dims['kv'], F32), (dims['kv'], F32)))
    qi, kw, u = _row_call(_proj_idx_kernel, big_tm, seq, x, (shift1, scale1),
                          (wts['norm1_g'], wts['w_qi'], wts['w_kw'], wts['w_u']),
                          ((dims['ih'] * dims['idim'], BF16), (LANES, F32), (dims['w'], F32)))
    a_n = attend(q, qi, kw, k, v)
    w = dims['w']
    u_tb = u.reshape(nb, seq, w).transpose(1, 0, 2).reshape(t, w)
    tc = min(64, seq)
    s_tb, h_re, h_im = _s5(u_tb, h0_re, h0_im, *ssm, r=nb, tc=tc)
    s_n = s_tb.reshape(seq, nb, w).transpose(1, 0, 2).reshape(t, w)
    x1, h2 = _out_proj(x, gate1, shift2, scale2, a_n, s_n, wts['w_out_a'], wts['w_out_s'], wts['norm2_g'],
                       tm=tm, rows_per_batch=seq)
    e, g = _peer_route(h2, peer['wq_t'], peer['k1'], peer['k2'], tb=tm)
    npick = e.shape[0] * e.shape[1]
    tb = min(64, tm)
    y = _peer_gather(e.reshape(npick, t).T, h2, g.reshape(npick, t), x1, gate2, peer['table'],
                     tb=tb, rows_per_batch=seq)
    return y, k, v, kw[:, :dims['idim']], h_re, h_im


def kernel(x_prompt, x_sample, cache_k, cache_v, cache_kidx, state_ssm_re, state_ssm_im, page_table, c_prompt, c_sample, w_ada, b_ada, norm1_g, w_in, q_gain, k_gain, ssm_A_re, ssm_A_im, ssm_B_re, ssm_B_im, ssm_C_re, ssm_C_im, ssm_D, ssm_log_dt, ssm_w_glu, ssm_b_glu, attn_out_g, ssm_out_g, w_out, norm2_g, peer_w_q, peer_sub_keys1, peer_sub_keys2, peer_u, peer_v):
    b, seq, d = x_prompt.shape
    db, dseq, _ = x_sample.shape
    assert dseq == 1
    hd = q_gain.shape[0]
    nkv = cache_k.shape[2]
    idim = cache_kidx.shape[2]
    g, p, c = ssm_B_re.shape
    w = g * c
    a = d - w
    nh = a // hd
    kv = nkv * hd
    ih = (w_in.shape[1] - a - 2 * kv - idim - w) // (idim + 1)
    assert a + 2 * kv + ih * idim + idim + ih + w == w_in.shape[1] and idim + ih <= LANES
    dims = dict(hd=hd, a=a, kv=kv, ih=ih, idim=idim, w=w)

    c_all = jnp.concatenate([c_prompt, c_sample], axis=0)
    pad = (-c_all.shape[0]) % SUBLANES
    c_all = jnp.pad(c_all, ((0, pad), (0, 0)))
    m = _adaln(c_all, w_ada, b_ada)
    mods_p = [mm[:b].reshape(b, 1, d) for mm in jnp.split(m, 6, axis=-1)]
    mods_s = [mm[b:b + db].reshape(1, db, d) for mm in jnp.split(m, 6, axis=-1)]

    cuts = [a, a + kv, a + 2 * kv, a + 2 * kv + ih * idim, a + 2 * kv + ih * idim + idim + ih]
    w_q, w_k, w_v, w_qi, w_kw, w_u = [x.astype(BF16) for x in jnp.split(w_in, cuts, axis=1)]
    w_kw = jnp.pad(w_kw, ((0, 0), (0, LANES - idim - ih)))
    row = lambda x: x.reshape(1, -1)
    wts = dict(norm1_g=row(norm1_g), w_q=w_q, w_k=w_k, w_v=w_v, w_qi=w_qi, w_kw=w_kw, w_u=w_u,
               q_gain=row(q_gain), k_gain=row(k_gain), w_out_a=w_out[:a].astype(BF16),
               w_out_s=w_out[a:].astype(BF16), norm2_g=row(norm2_g))

    gp_shape = jax.ShapeDtypeStruct((g, p), F32)
    cgp_shape = jax.ShapeDtypeStruct((c, g, p), F32)
    ab_re, ab_im, bb_re, bb_im = pl.pallas_call(
        _ssm_param_kernel, out_shape=[gp_shape, gp_shape, cgp_shape, cgp_shape], compiler_params=_cp())(
            ssm_A_re, ssm_A_im, ssm_log_dt.reshape(g, 1), ssm_B_re.transpose(2, 0, 1), ssm_B_im.transpose(2, 0, 1))
    ns = max(1, g // SSM_SLAB_GROUPS)
    ssm = (ab_re.reshape(1, g * p), ab_im.reshape(1, g * p),
           _block_diag(bb_re.transpose(1, 0, 2), ns).astype(BF16), _block_diag(bb_im.transpose(1, 0, 2), ns).astype(BF16),
           _block_diag(ssm_C_re.transpose(0, 2, 1), ns).astype(BF16),
           _block_diag(ssm_C_im.transpose(0, 2, 1), ns).astype(BF16),
           ssm_D.reshape(1, w), ssm_w_glu.astype(BF16), row(ssm_b_glu), row(ssm_out_g))

    hp, nk, half = peer_sub_keys1.shape
    peer = dict(wq_t=peer_w_q.T.astype(BF16), k1=peer_sub_keys1.astype(BF16), k2=peer_sub_keys2.astype(BF16),
                table=_expert_slabs(peer_u, peer_v))
    og = row(attn_out_g)

    attend_p = functools.partial(_prompt_attention, og=og, b=b, nh=nh, nkv=nkv, hd=hd, ih=ih, idim=idim)
    attend_s = functools.partial(_sample_attention, og=og, cache_k=cache_k, cache_v=cache_v, cache_kidx=cache_kidx,
                                 page_table=page_table, nh=nh, nkv=nkv, hd=hd, ih=ih, idim=idim)
    zeros = jnp.zeros((b, g * p), F32)
    yp, k_p, v_p, ki_p, re_p, im_p = _trunk(x_prompt.reshape(b * seq, d), mods_p, wts, ssm, peer, zeros, zeros,
                                            attend_p, nb=b, seq=seq, dims=dims)
    ys, k_s, v_s, ki_s, re_s, im_s = _trunk(x_sample.reshape(db, d), mods_s, wts, ssm, peer,
                                            state_ssm_re.reshape(db, g * p), state_ssm_im.reshape(db, g * p),
                                            attend_s, nb=db, seq=1, dims=dims)
    return (yp.reshape(b, seq, d), ys.reshape(db, 1, d),
            k_p.reshape(b, seq, nkv, hd), v_p.reshape(b, seq, nkv, hd), ki_p.reshape(b, seq, idim),
            re_p.reshape(b, g, p), im_p.reshape(b, g, p),
            k_s.reshape(db, 1, nkv, hd), v_s.reshape(db, 1, nkv, hd), ki_s.reshape(db, 1, idim),
            re_s.reshape(db, g, p), im_s.reshape(db, g, p))
```

```python
import functools

import jax
import jax.numpy as jnp
from jax import lax
from jax.experimental import pallas as pl
from jax.experimental.pallas import tpu as pltpu

F32 = jnp.float32
BF16 = jnp.bfloat16
I32 = jnp.int32

RMS_EPS = 1e-6
TOPK_MAX = 256
PEER_TOPK = 16
LANES = 128
SUBLANES = 8
SSM_SLAB_GROUPS = 16
INT_MIN = -(2 ** 31)
VMEM_LIMIT = 56 * 1024 * 1024


def _cp(*sem, vmem=VMEM_LIMIT, **kw):
    return pltpu.CompilerParams(dimension_semantics=sem, vmem_limit_bytes=vmem, **kw)


def _nt_dot(a, b):
    return lax.dot_general(a, b, (((1,), (1,)), ((), ())), preferred_element_type=F32)


def _dot(a, b):
    return jnp.dot(a, b, preferred_element_type=F32)


def _rms(x, g):
    return x * lax.rsqrt(jnp.mean(x * x, axis=-1, keepdims=True) + RMS_EPS) * g


def _mod_spec(mod, tm, rows_per_batch):
    _, r, d = mod.shape
    if r == 1:
        return pl.BlockSpec((1, 1, d), lambda i, *_: ((i * tm) // rows_per_batch, 0, 0))
    assert r == tm
    return pl.BlockSpec((1, r, d), lambda i, *_: (0, 0, 0))


def _adaln_kernel(c_ref, w_ref, b_ref, o_ref):
    c = c_ref[...]
    s = (c * jax.nn.sigmoid(c)).astype(BF16)
    o_ref[...] = _dot(s, w_ref[...].astype(BF16)) + b_ref[...]


def _adaln(c, w, b):
    m, d = c.shape
    n = w.shape[1]
    tn = 512
    return pl.pallas_call(
        _adaln_kernel, grid=(n // tn,),
        in_specs=[pl.BlockSpec((m, d), lambda j: (0, 0)), pl.BlockSpec((d, tn), lambda j: (0, j)),
                  pl.BlockSpec((1, tn), lambda j: (0, j))],
        out_specs=pl.BlockSpec((m, tn), lambda j: (0, j)),
        out_shape=jax.ShapeDtypeStruct((m, n), F32), compiler_params=_cp("parallel"))(c, w, b.reshape(1, n))


def _head_rms_store(o_ref, p, gain, hd):
    for h in range(p.shape[1] // hd):
        o_ref[:, h * hd:(h + 1) * hd] = _rms(p[:, h * hd:(h + 1) * hd], gain).astype(o_ref.dtype)


def _proj_qkv_kernel(x_ref, sh_ref, sc_ref, g_ref, wq_ref, wk_ref, wv_ref, qg_ref, kg_ref,
                     q_ref, k_ref, v_ref, *, hd):
    hb = (_rms(x_ref[...], g_ref[...]) * (1.0 + sc_ref[0]) + sh_ref[0]).astype(BF16)
    _head_rms_store(q_ref, _dot(hb, wq_ref[...]), qg_ref[...], hd)
    _head_rms_store(k_ref, _dot(hb, wk_ref[...]), kg_ref[...], hd)
    v_ref[...] = _dot(hb, wv_ref[...])


def _proj_idx_kernel(x_ref, sh_ref, sc_ref, g_ref, wqi_ref, wkw_ref, wu_ref, qi_ref, kw_ref, u_ref):
    hb = (_rms(x_ref[...], g_ref[...]) * (1.0 + sc_ref[0]) + sh_ref[0]).astype(BF16)
    qi_ref[...] = _dot(hb, wqi_ref[...]).astype(qi_ref.dtype)
    kw_ref[...] = _dot(hb, wkw_ref[...])
    u_ref[...] = _dot(hb, wu_ref[...])


def _row_call(kernel, tm, rows_per_batch, x, mods, consts, outs):
    t, d = x.shape
    full = lambda a: pl.BlockSpec(a.shape, lambda i: (0,) * a.ndim)
    return pl.pallas_call(
        kernel, grid=(t // tm,),
        in_specs=[pl.BlockSpec((tm, d), lambda i: (i, 0))] + [_mod_spec(m, tm, rows_per_batch) for m in mods]
        + [full(c) for c in consts],
        out_specs=[pl.BlockSpec((tm, n), lambda i: (i, 0)) for n, _ in outs],
        out_shape=[jax.ShapeDtypeStruct((t, n), dt) for n, dt in outs],
        compiler_params=_cp("parallel"))(x, *mods, *consts)


def _ordinal_to_float(o):
    return lax.bitcast_convert_type(jnp.where(o < 0, o ^ 0x7FFFFFFF, o), F32)


def _topk_mask(s, col, k, extra=None, ncols=None):
    m = s.shape[0]
    kf = float(k)
    count = lambda pred: jnp.sum(pred.astype(F32), axis=-1, keepdims=True)

    def bit_step(it, o):
        cand = o + lax.shift_left(jnp.int32(1), 31 - it)
        c = _ordinal_to_float(cand)
        n = count(s >= c)
        if extra is not None:
            n = n + (extra >= c).astype(F32)
        return jnp.where(n >= kf, cand, o)

    v = _ordinal_to_float(lax.fori_loop(0, 32, bit_step, jnp.full((m, 1), INT_MIN, I32)))
    gt = s > v
    eq = s == v
    n_gt = count(gt)
    if extra is not None:
        n_gt = n_gt + (extra > v).astype(F32)
    need = kf - n_gt
    nbits = max(1, (ncols - 1).bit_length())

    def col_step(it, lim):
        cand = lim + lax.shift_left(jnp.int32(1), nbits - 1 - it)
        return jnp.where(count(eq & (col < cand)) < need, cand, lim)

    lim = lax.fori_loop(0, nbits, col_step, jnp.zeros((m, 1), I32))
    finite = s > -jnp.inf
    few = count(finite) + (0.0 if extra is None else 1.0) <= kf
    mask = (few & finite) | (jnp.logical_not(few) & (gt | (eq & (col <= lim))))
    if extra is None:
        return mask, None
    extra_sel = few | (extra > v) | ((extra == v) & (count(eq) < need))
    return mask, extra_sel


def _prompt_attn_kernel(q_ref, qi_ref, kwq_ref, k_ref, v_ref, kwall_ref, og_ref, o_ref, acc_ref,
                        *, nh, nkv, hd, ih, idim, topk, q0):
    tq = q_ref.shape[0]
    s_len = k_ref.shape[1]
    i = pl.program_id(1) + q0
    kib = kwall_ref[0, :, :idim].astype(BF16)
    w = kwq_ref[:, idim:idim + ih]
    qi = qi_ref[...]
    sc = jnp.zeros((tq, s_len), F32)
    for h in range(ih):
        r = _nt_dot(qi[:, h * idim:(h + 1) * idim], kib)
        sc = sc + jnp.maximum(r, 0.0) * w[:, h:h + 1]
    col = lax.broadcasted_iota(I32, (tq, s_len), 1)
    row = lax.broadcasted_iota(I32, (tq, s_len), 0) + i * tq
    adm = col <= row
    sel, _ = _topk_mask(jnp.where(adm, sc, -jnp.inf), col, topk, ncols=s_len)
    kb = k_ref[0].astype(BF16)
    vb = v_ref[0].astype(BF16)
    g = nh // nkv
    scale = hd ** -0.5
    for hk in range(nkv):
        kh = kb[:, hk * hd:(hk + 1) * hd]
        vh = vb[:, hk * hd:(hk + 1) * hd]
        for j in range(g):
            h = hk * g + j
            s = _nt_dot(q_ref[:, h * hd:(h + 1) * hd], kh) * scale
            s = jnp.where(sel, s, -jnp.inf)
            p = jnp.exp(s - jnp.max(s, axis=-1, keepdims=True))
            l = jnp.sum(p, axis=-1, keepdims=True)
            acc_ref[:, h * hd:(h + 1) * hd] = _dot(p.astype(BF16), vh) / l
    o_ref[0] = _rms(acc_ref[...], og_ref[...]).astype(o_ref.dtype)


PROMPT_KEY_CLASSES = 8


def _prompt_attention(q, qi, kw, k, v, og, *, b, nh, nkv, hd, ih, idim):
    t, a = q.shape
    s_len = t // b
    tq = min(128, s_len)
    nq = s_len // tq
    topk = min(TOPK_MAX, s_len // 4)
    ncls = min(PROMPT_KEY_CLASSES, nq)
    per = nq // ncls
    assert per * ncls == nq
    k3, v3, kw3 = (x.reshape(b, s_len, x.shape[1]) for x in (k, v, kw))
    outs = []
    for c in range(ncls):
        q0 = c * per
        sk = (c + 1) * per * tq
        kern = functools.partial(_prompt_attn_kernel, nh=nh, nkv=nkv, hd=hd, ih=ih, idim=idim, topk=topk, q0=q0)
        blk = lambda n, q0=q0: pl.BlockSpec((tq, n), lambda bi, i: (bi * nq + q0 + i, 0))
        keys = lambda n, sk=sk: pl.BlockSpec((1, sk, n), lambda bi, i: (bi, 0, 0))
        outs.append(pl.pallas_call(
            kern, grid=(b, per),
            in_specs=[blk(a), blk(qi.shape[1]), blk(kw.shape[1]), keys(k.shape[1]), keys(v.shape[1]),
                      keys(kw.shape[1]), pl.BlockSpec((1, a), lambda bi, i: (0, 0))],
            out_specs=pl.BlockSpec((1, tq, a), lambda bi, i: (bi, i, 0)),
            out_shape=jax.ShapeDtypeStruct((b, per * tq, a), BF16),
            scratch_shapes=[pltpu.VMEM((tq, a), F32)],
            compiler_params=_cp("parallel", "parallel"))(q, qi, kw, k3, v3, kw3, og))
    return jnp.concatenate(outs, axis=1).reshape(t, a)


def _sample_score_kernel(pt_ref, qi_ref, w_ref, kn_ref, kc_ref, o_ref, sn_ref, buf_ref, sem_ref):
    db, npg = pt_ref.shape
    idim, page = kc_ref.shape[1], kc_ref.shape[2]

    def page_copy(page_id, slot, pg):
        return pltpu.make_async_copy(kc_ref.at[page_id],
                                     buf_ref.at[slot, :, pl.ds(pl.multiple_of(pg * page, page), page)],
                                     sem_ref.at[slot])

    def issue(b, slot):
        def body(pg, c):
            page_copy(pt_ref[b, pg], slot, pg).start()
            return c
        lax.fori_loop(0, npg, body, 0, unroll=8)

    def wait(slot):
        def body(pg, c):
            page_copy(0, slot, pg).wait()
            return c
        lax.fori_loop(0, npg, body, 0, unroll=8)

    issue(0, 0)

    def row(b, c):
        slot = b % 2

        @pl.when(b + 1 < db)
        def _():
            issue(b + 1, 1 - slot)

        wait(slot)
        qi = qi_ref[b]
        w = w_ref[b]
        r = _dot(qi, buf_ref[slot].astype(BF16))
        o_ref[pl.ds(b, 1), :] = jnp.sum(jnp.maximum(r, 0.0) * w, axis=0, keepdims=True)
        rn = _nt_dot(qi, jnp.broadcast_to(kn_ref[b], (LANES, idim)).astype(BF16))
        sn_ref[pl.ds(b, 1), :] = jnp.sum(jnp.maximum(rn, 0.0) * w, axis=0, keepdims=True)
        return c

    lax.fori_loop(0, db, row, 0)


def _sample_select_kernel(sc_ref, sn_ref, m_ref, ns_ref, *, topk):
    sc = sc_ref[...]
    db, past = sc.shape
    col = lax.broadcasted_iota(I32, (db, past), 1)
    mask, new_sel = _topk_mask(sc, col, topk, extra=sn_ref[:, :1], ncols=past)
    m_ref[...] = mask.astype(F32)
    ns_ref[...] = jnp.broadcast_to(new_sel.astype(F32), ns_ref.shape)


def _sample_attn_kernel(pt_ref, q_ref, m_ref, ns_ref, kn_ref, vn_ref, kc_ref, vc_ref, og_ref, o_ref,
                        kbuf_ref, vbuf_ref, sem_ref, mx_ref, l_ref, acc_ref, *, nh, nkv, hd):
    db, npg = pt_ref.shape
    cp = kbuf_ref.shape[1]
    page = kbuf_ref.shape[2] // nkv
    nchunk = npg // cp
    total = db * nchunk
    g = nh // nkv
    scale = hd ** -0.5

    def page_copies(page_id, slot, pg):
        return (pltpu.make_async_copy(kc_ref.at[page_id], kbuf_ref.at[slot, pg], sem_ref.at[0, slot]),
                pltpu.make_async_copy(vc_ref.at[page_id], vbuf_ref.at[slot, pg], sem_ref.at[1, slot]))

    def issue(gi, slot):
        b = gi // nchunk
        c = gi % nchunk
        for pg in range(cp):
            for cpy in page_copies(pt_ref[b, c * cp + pg], slot, pg):
                cpy.start()

    def wait(slot):
        for pg in range(cp):
            for cpy in page_copies(0, slot, pg):
                cpy.wait()

    def per_kv_head(fn, n):
        head_kv = lax.broadcasted_iota(I32, (nh, n), 0) // g
        out = fn(0)
        for hk in range(1, nkv):
            out = jnp.where(head_kv == hk, fn(hk), out)
        return out

    def update(qb, key_of, val_of, n, keep):
        s = per_kv_head(lambda hk: _nt_dot(qb, key_of(hk)), n) * scale
        s = jnp.where(keep, s, -jnp.inf)
        m_old = mx_ref[...]
        m_new = jnp.maximum(m_old, jnp.max(s, axis=-1, keepdims=True))
        m_safe = jnp.where(m_new == -jnp.inf, 0.0, m_new)
        alpha = jnp.exp(m_old - m_safe)
        p = jnp.exp(s - m_safe)
        pb = p.astype(BF16)
        l_ref[...] = alpha * l_ref[...] + jnp.sum(p, axis=-1, keepdims=True)
        acc_ref[...] = alpha * acc_ref[...] + per_kv_head(lambda hk: _dot(pb, val_of(hk)), hd)
        mx_ref[...] = m_new

    issue(0, 0)

    def step(gi, carry):
        slot = gi % 2
        b = gi // nchunk
        c = gi % nchunk

        @pl.when(gi + 1 < total)
        def _():
            issue(gi + 1, 1 - slot)

        wait(slot)

        @pl.when(c == 0)
        def _():
            mx_ref[...] = jnp.full(mx_ref.shape, -jnp.inf, F32)
            l_ref[...] = jnp.zeros(l_ref.shape, F32)
            acc_ref[...] = jnp.zeros(acc_ref.shape, F32)

        qb = q_ref[b].astype(BF16)
        head_rows = lambda buf, hk: buf[slot, :, pl.ds(hk, page, stride=nkv), :].reshape(cp * page, hd).astype(BF16)
        update(qb, lambda hk: head_rows(kbuf_ref, hk), lambda hk: head_rows(vbuf_ref, hk), cp * page,
               m_ref[b, c] > 0.0)

        @pl.when(c == nchunk - 1)
        def _():
            n8 = kn_ref.shape[1]
            first = lax.broadcasted_iota(I32, (1, n8), 1) == 0
            kn = kn_ref[b].astype(BF16)
            vn = vn_ref[b].astype(BF16)
            update(qb, lambda hk: kn[:, hk * hd:(hk + 1) * hd], lambda hk: vn[:, hk * hd:(hk + 1) * hd], n8,
                   first & (ns_ref[pl.ds(b, 1), :1] > 0.0))
            o = acc_ref[...] / l_ref[...]
            ms = jnp.sum(jnp.sum(o * o, axis=-1, keepdims=True), axis=0, keepdims=True) / (nh * hd)
            o_ref[b] = o * lax.rsqrt(ms + RMS_EPS) * og_ref[...]

        return carry

    lax.fori_loop(0, total, step, 0)


SAMPLE_PAGES_PER_CHUNK = 16


def _sample_attention(q, qi, kw, k, v, og, cache_k, cache_v, cache_kidx, page_table, *, nh, nkv, hd, ih, idim):
    db = q.shape[0]
    npg = page_table.shape[1]
    page = cache_k.shape[1]
    past = npg * page
    topk = min(TOPK_MAX, (past + 1) // 4)
    qi3 = qi.reshape(db, ih, idim)
    w3 = kw[:, idim:idim + ih].reshape(db, ih, 1)
    vmem = pl.BlockSpec(memory_space=pltpu.VMEM)
    smem = pl.BlockSpec(memory_space=pltpu.SMEM)
    hbm = pl.BlockSpec(memory_space=pl.ANY)
    scores, s_new = pl.pallas_call(
        _sample_score_kernel,
        in_specs=[smem, vmem, vmem, vmem, hbm], out_specs=[vmem, vmem],
        out_shape=[jax.ShapeDtypeStruct((db, past), F32), jax.ShapeDtypeStruct((db, LANES), F32)],
        scratch_shapes=[pltpu.VMEM((2, idim, past), F32), pltpu.SemaphoreType.DMA((2,))],
        compiler_params=_cp())(page_table, qi3, w3, kw[:, :idim].reshape(db, 1, idim),
                               cache_kidx.transpose(0, 2, 1))
    mask, new_sel = pl.pallas_call(
        functools.partial(_sample_select_kernel, topk=topk),
        out_shape=[jax.ShapeDtypeStruct((db, past), F32), jax.ShapeDtypeStruct((db, LANES), F32)],
        compiler_params=_cp())(scores, s_new)
    cp = min(SAMPLE_PAGES_PER_CHUNK, npg)
    assert npg % cp == 0
    rep = lambda a: jnp.broadcast_to(a[:, None, :], (db, SUBLANES, a.shape[1]))
    out = pl.pallas_call(
        functools.partial(_sample_attn_kernel, nh=nh, nkv=nkv, hd=hd),
        in_specs=[smem, vmem, vmem, vmem, vmem, vmem, hbm, hbm, vmem], out_specs=vmem,
        out_shape=jax.ShapeDtypeStruct((db, nh, hd), F32),
        scratch_shapes=[pltpu.VMEM((2, cp, page * nkv, hd), F32), pltpu.VMEM((2, cp, page * nkv, hd), F32),
                        pltpu.SemaphoreType.DMA((2, 2)),
                        pltpu.VMEM((nh, 1), F32), pltpu.VMEM((nh, 1), F32), pltpu.VMEM((nh, hd), F32)],
        compiler_params=_cp())(
            page_table, q.astype(F32).reshape(db, nh, hd), mask.reshape(db, npg // cp, 1, cp * page), new_sel,
            rep(k), rep(v), cache_k.reshape(-1, page * nkv, hd), cache_v.reshape(-1, page * nkv, hd),
            og.reshape(nh, hd))
    return out.reshape(db, nh * hd).astype(BF16)


def _ssm_param_kernel(are_ref, aim_ref, ldt_ref, bre_ref, bim_ref, abre_ref, abim_ref, bbre_ref, bbim_ref):
    a_re = are_ref[...]
    a_im = aim_ref[...]
    dt = jnp.exp(ldt_ref[...])
    mag = jnp.exp(a_re * dt)
    ab_re = mag * jnp.cos(a_im * dt)
    ab_im = mag * jnp.sin(a_im * dt)
    den = a_re * a_re + a_im * a_im
    nr = ab_re - 1.0
    co_re = (nr * a_re + ab_im * a_im) / den
    co_im = (ab_im * a_re - nr * a_im) / den
    abre_ref[...] = ab_re
    abim_ref[...] = ab_im
    b_re = bre_ref[...]
    b_im = bim_ref[...]
    bbre_ref[...] = co_re[None] * b_re - co_im[None] * b_im
    bbim_ref[...] = co_re[None] * b_im + co_im[None] * b_re


def _s5_kernel(u_ref, h0re_ref, h0im_ref, ar_ref, ai_ref, wre_ref, wim_ref, cre_ref, cim_ref, d_ref,
               wglu_ref, bglu_ref, og_ref, s_ref, hre_ref, him_ref, bure_ref, buim_ref, y_ref, *, r, tc):
    ci = pl.program_id(0)
    ns, sc_w, sp_w = wre_ref.shape

    @pl.when(ci == 0)
    def _():
        hre_ref[...] = h0re_ref[...]
        him_ref[...] = h0im_ref[...]

    for s in range(ns):
        cols = slice(s * sc_w, (s + 1) * sc_w)
        st = slice(s * sp_w, (s + 1) * sp_w)
        ub = u_ref[:, cols].astype(BF16)
        bure_ref[...] = _dot(ub, wre_ref[s])
        buim_ref[...] = _dot(ub, wim_ref[s])
        a_r = jnp.broadcast_to(ar_ref[:, st], (r, sp_w))
        a_i = jnp.broadcast_to(ai_ref[:, st], (r, sp_w))

        def step(t, carry):
            hr, hi = carry
            rows = pl.ds(pl.multiple_of(t * r, r), r)
            nr = a_r * hr - a_i * hi + bure_ref[rows, :]
            ni = a_r * hi + a_i * hr + buim_ref[rows, :]
            bure_ref[rows, :] = nr
            buim_ref[rows, :] = ni
            return nr, ni

        hr, hi = lax.fori_loop(0, tc, step, (hre_ref[:, st], him_ref[:, st]))
        hre_ref[:, st] = hr
        him_ref[:, st] = hi
        y_ref[:, cols] = (_dot(bure_ref[...].astype(BF16), cre_ref[s])
                          - _dot(buim_ref[...].astype(BF16), cim_ref[s]))
    y = jax.nn.gelu(y_ref[...] + d_ref[...] * u_ref[...])
    z = _dot(y.astype(BF16), wglu_ref[...]) + bglu_ref[...]
    s_ref[...] = _rms(y * jax.nn.sigmoid(z), og_ref[...]).astype(s_ref.dtype)


def _s5(u_tb, h0_re, h0_im, ab_re, ab_im, w_re, w_im, c_re, c_im, d, w_glu, b_glu, og, *, r, tc):
    rows, w = u_tb.shape
    gp = h0_re.shape[1]
    full = lambda a: pl.BlockSpec(a.shape, lambda i: (0,) * a.ndim)
    consts = (h0_re, h0_im, ab_re, ab_im, w_re, w_im, c_re, c_im, d, w_glu, b_glu, og)
    sp_w = w_re.shape[2]
    return pl.pallas_call(
        functools.partial(_s5_kernel, r=r, tc=tc), grid=(rows // (r * tc),),
        in_specs=[pl.BlockSpec((r * tc, w), lambda i: (i, 0))] + [full(c) for c in consts],
        out_specs=[pl.BlockSpec((r * tc, w), lambda i: (i, 0)), full(h0_re), full(h0_im)],
        out_shape=[jax.ShapeDtypeStruct((rows, w), BF16), jax.ShapeDtypeStruct((r, gp), F32),
                   jax.ShapeDtypeStruct((r, gp), F32)],
        scratch_shapes=[pltpu.VMEM((r * tc, sp_w), F32), pltpu.VMEM((r * tc, sp_w), F32),
                        pltpu.VMEM((r * tc, w), F32)],
        compiler_params=_cp("arbitrary"))(u_tb, *consts)


def _out_proj_kernel(x_ref, g1_ref, sh_ref, sc_ref, a_ref, s_ref, wa_ref, ws_ref, n2_ref, x1_ref, h2_ref):
    mix = _dot(a_ref[...], wa_ref[...]) + _dot(s_ref[...], ws_ref[...])
    x1 = x_ref[...] + g1_ref[0] * mix
    x1_ref[...] = x1
    h2_ref[...] = _rms(x1, n2_ref[...]) * (1.0 + sc_ref[0]) + sh_ref[0]


def _out_proj(x, gate1, shift2, scale2, a_n, s_n, w_a, w_s, n2, *, tm, rows_per_batch):
    t, d = x.shape
    full = lambda a: pl.BlockSpec(a.shape, lambda i: (0,) * a.ndim)
    rowblk = lambda n: pl.BlockSpec((tm, n), lambda i: (i, 0))
    return pl.pallas_call(
        _out_proj_kernel, grid=(t // tm,),
        in_specs=[rowblk(d)] + [_mod_spec(m, tm, rows_per_batch) for m in (gate1, shift2, scale2)]
        + [rowblk(a_n.shape[1]), rowblk(s_n.shape[1]), full(w_a), full(w_s), full(n2)],
        out_specs=[rowblk(d), rowblk(d)],
        out_shape=[jax.ShapeDtypeStruct((t, d), F32), jax.ShapeDtypeStruct((t, d), F32)],
        compiler_params=_cp("parallel"))(x, gate1, shift2, scale2, a_n, s_n, w_a, w_s, n2)


def _take_top(s, order, k, pay=None):
    big = 3.0e38
    kid = lax.broadcasted_iota(I32, (k, s.shape[1]), 0)
    vals = jnp.zeros((k, s.shape[1]), F32)
    picks = jnp.zeros((k, s.shape[1]), F32)
    for it in range(k):
        m = jnp.max(s, axis=0, keepdims=True)
        first = jnp.min(jnp.where(s == m, order, big), axis=0, keepdims=True)
        hit = order == first
        pv = first if pay is None else jnp.max(jnp.where(hit, pay, -1.0), axis=0, keepdims=True)
        vals = jnp.where(kid == it, m, vals)
        picks = jnp.where(kid == it, pv, picks)
        s = jnp.where(hit, -jnp.inf, s)
    return vals, picks


def _pair_candidates(v1, i1, v2, i2, nk):
    k, tb = v1.shape
    assert k == 16
    widths = [k] + [SUBLANES] * (k // 2 - 1)
    row = lambda w: lax.broadcasted_iota(I32, (w, tb), 0).astype(F32)
    sums = [v1[a:a + 1, :] + v2[:w, :] for a, w in enumerate(widths)]
    flat = [row(w) + float(a * k) for a, w in enumerate(widths)]
    ids = [i1[a:a + 1, :] * float(nk) + i2[:w, :] for a, w in enumerate(widths)]
    sums.append(v1[k // 2:, :] + v2[:1, :])
    flat.append((row(k // 2) + float(k // 2)) * float(k))
    ids.append(i1[k // 2:, :] * float(nk) + i2[:1, :])
    return jnp.concatenate(sums, axis=0), jnp.concatenate(flat, axis=0), jnp.concatenate(ids, axis=0)


def _peer_route_kernel(h_ref, wq_ref, k1_ref, k2_ref, e_ref, g_ref, *, nk):
    half = k1_ref.shape[2]
    tb = h_ref.shape[0]
    k = PEER_TOPK
    qt = _nt_dot(wq_ref[...], h_ref[...].astype(BF16))
    s1 = _dot(k1_ref[0], qt[:half, :].astype(BF16))
    s2 = _dot(k2_ref[0], qt[half:, :].astype(BF16))
    rid = lax.broadcasted_iota(I32, (nk, tb), 0).astype(F32)
    v1, i1 = _take_top(s1, rid, k)
    v2, i2 = _take_top(s2, rid, k)
    cand, flat, ids = _pair_candidates(v1, i1, v2, i2, nk)
    sc, e = _take_top(cand, flat, k, pay=ids)
    p = jnp.exp(sc - jnp.max(sc, axis=0, keepdims=True))
    g_ref[0] = p / jnp.sum(p, axis=0, keepdims=True)
    e_ref[0] = e.astype(I32)


def _peer_route(h2, wq_t, k1, k2, *, tb):
    t, d = h2.shape
    hp, nk, half = k1.shape
    return pl.pallas_call(
        functools.partial(_peer_route_kernel, nk=nk), grid=(t // tb, hp),
        in_specs=[pl.BlockSpec((tb, d), lambda i, h: (i, 0)), pl.BlockSpec((2 * half, d), lambda i, h: (h, 0)),
                  pl.BlockSpec((1, nk, half), lambda i, h: (h, 0, 0)),
                  pl.BlockSpec((1, nk, half), lambda i, h: (h, 0, 0))],
        out_specs=[pl.BlockSpec((1, PEER_TOPK, tb), lambda i, h: (h, 0, i)),
                   pl.BlockSpec((1, PEER_TOPK, tb), lambda i, h: (h, 0, i))],
        out_shape=[jax.ShapeDtypeStruct((hp, PEER_TOPK, t), I32), jax.ShapeDtypeStruct((hp, PEER_TOPK, t), F32)],
        compiler_params=_cp("parallel", "arbitrary"))(h2, wq_t, k1, k2)


PEER_GATHER_BATCH = 2
PEER_RING_GROUPS = 4
PEER_RING_AHEAD = PEER_RING_GROUPS - 1


def _peer_gather_kernel(e_ref, en_ref, h_ref, g_ref, x1_ref, gate_ref, tab_ref, y_ref, *scratch):
    *bufs, wgt_ref, sem_ref = scratch
    nbt = PEER_GATHER_BATCH
    ngrp = len(bufs) // nbt
    ahead = PEER_RING_AHEAD
    tb = h_ref.shape[0]
    npick = bufs[0].shape[0]
    R = bufs[0].shape[1] // 2
    i = pl.program_id(0)

    def slab_copy(e, slot, p):
        return pltpu.make_async_copy(tab_ref.at[e], bufs[slot].at[p], sem_ref.at[slot])

    def issue(idx_ref, j, slot):
        for p in range(npick):
            slab_copy(idx_ref[j, p], slot, p).start(priority=p % 2)

    def wait(slot):
        pltpu.make_async_copy(tab_ref.at[pl.ds(0, npick)], bufs[slot], sem_ref.at[slot]).wait()

    group_sum = (lax.broadcasted_iota(I32, (npick, npick * R), 1) // R
                 == lax.broadcasted_iota(I32, (npick, npick * R), 0)).astype(BF16)
    gw = g_ref.shape[1]
    g_lane = lax.broadcasted_iota(I32, (npick, gw), 1)

    def compute(j, slot):
        buf = bufs[slot]
        xb = h_ref[j]
        prods = [buf[p, :R, :] * xb for p in range(npick)]
        hid = jnp.sum(_dot(group_sum, jnp.concatenate(prods, axis=0)), axis=-1, keepdims=True)
        jcol = (i * tb) % gw + j
        gcol = jnp.sum(jnp.where(g_lane == jcol, g_ref[...], 0.0), axis=-1, keepdims=True)
        wgt_ref[slot % nbt] = jnp.broadcast_to(gcol * jax.nn.gelu(hid), (npick, LANES))
        nacc = 4
        acc = [jnp.zeros((R, LANES), F32) for _ in range(nacc)]
        for p in range(npick):
            w_p = jnp.broadcast_to(wgt_ref[slot % nbt, p:p + 1, :], (R, LANES))
            acc[p % nacc] = acc[p % nacc] + w_p * buf[p, R:, :].astype(F32)
        out = sum(acc[1:], acc[0])
        row = pl.ds(j, 1)
        gate = gate_ref[0] if gate_ref.shape[1] == 1 else gate_ref[0, row, :]
        out_row = jnp.concatenate([out[r:r + 1, :] for r in range(R)], axis=1)
        y_ref[row, :] = x1_ref[row, :] + gate * out_row

    @pl.when(i == 0)
    def _():
        for s in range(ahead * nbt):
            issue(e_ref, s, s)

    def ring_turn(k, c):
        for grp in range(ngrp):
            j0 = (ngrp * k + grp) * nbt
            for s in range(nbt):
                wait(grp * nbt + s)
            for s in range(nbt):
                issue(en_ref, j0 + s, ((grp + ahead) % ngrp) * nbt + s)
            for s in range(nbt):
                compute(j0 + s, grp * nbt + s)
        return c

    lax.fori_loop(0, tb // (ngrp * nbt), ring_turn, 0)

    @pl.when(i == pl.num_programs(0) - 1)
    def _():
        for s in range(ahead * nbt):
            wait(s)


def _peer_gather(e_tok, h2, g_t, x1, gate2, table, *, tb, rows_per_batch):
    t, d = h2.shape
    npick = e_tok.shape[1]
    rows = table.shape[1] // 2
    assert rows * LANES == d
    nbt = PEER_GATHER_BATCH
    nbuf = PEER_RING_GROUPS * nbt
    assert tb % nbuf == 0 and t % tb == 0
    rowblk = pl.BlockSpec((tb, d), lambda i: (i, 0))
    idxblk = pl.BlockSpec((tb, npick), lambda i: (i, 0), memory_space=pltpu.SMEM)
    gw = t if t < LANES else max(tb, LANES)
    assert gw % tb == 0 and t % gw == 0
    return pl.pallas_call(
        _peer_gather_kernel, grid=(t // tb,),
        in_specs=[idxblk, idxblk, pl.BlockSpec((tb, rows, LANES), lambda i: (i, 0, 0)),
                  pl.BlockSpec((npick, gw), lambda i: (0, (i * tb) // gw)), rowblk,
                  _mod_spec(gate2, tb, rows_per_batch), pl.BlockSpec(memory_space=pl.ANY)],
        out_specs=rowblk, out_shape=jax.ShapeDtypeStruct((t, d), F32),
        scratch_shapes=[pltpu.VMEM((npick, 2 * rows, LANES), BF16)] * nbuf
        + [pltpu.VMEM((nbt, npick, LANES), F32), pltpu.SemaphoreType.DMA((nbuf,))],
        compiler_params=_cp("arbitrary", disable_bounds_checks=True))(
            e_tok, jnp.roll(e_tok, -PEER_RING_AHEAD * nbt, axis=0), h2.astype(BF16).reshape(t, rows, LANES), g_t, x1,
            gate2, table)


def _expert_slabs(u, v):
    e, d = u.shape
    r = d // LANES
    assert r * LANES == d and r % (2 * SUBLANES) == 0
    return jnp.concatenate([u.astype(BF16).reshape(e, r, LANES), v.astype(BF16).reshape(e, r, LANES)], axis=1)


def _block_diag(m, ns):
    g, a, b = m.shape
    gs = g // ns
    eye = jnp.eye(gs, dtype=m.dtype)
    return jnp.einsum('sgab,gh->sgahb', m.reshape(ns, gs, a, b), eye).reshape(ns, gs * a, gs * b)


def _trunk(x, mods, wts, ssm, peer, h0_re, h0_im, attend, *, nb, seq, dims):
    t, d = x.shape
    shift1, scale1, gate1, shift2, scale2, gate2 = mods
    per_batch_mod = shift1.shape[1] == 1
    tm = min(256, seq) if per_batch_mod else t
    big_tm = min(512, seq) if per_batch_mod else t
    q, k, v = _row_call(functools.partial(_proj_qkv_kernel, hd=dims['hd']), big_tm, seq, x, (shift1, scale1),
                        (wts['norm1_g'], wts['w_q'], wts['w_k'], wts['w_v'], wts['q_gain'], wts['k_gain']),
                        ((dims['a'], BF16), (dims['kv'], F32), (dims['kv'], F32)))
    qi, kw, u = _row_call(_proj_idx_kernel, big_tm, seq, x, (shift1, scale1),
                          (wts['norm1_g'], wts['w_qi'], wts['w_kw'], wts['w_u']),
                          ((dims['ih'] * dims['idim'], BF16), (LANES, F32), (dims['w'], F32)))
    a_n = attend(q, qi, kw, k, v)
    w = dims['w']
    u_tb = u.reshape(nb, seq, w).transpose(1, 0, 2).reshape(t, w)
    tc = min(64, seq)
    s_tb, h_re, h_im = _s5(u_tb, h0_re, h0_im, *ssm, r=nb, tc=tc)
    s_n = s_tb.reshape(seq, nb, w).transpose(1, 0, 2).reshape(t, w)
    x1, h2 = _out_proj(x, gate1, shift2, scale2, a_n, s_n, wts['w_out_a'], wts['w_out_s'], wts['norm2_g'],
                       tm=tm, rows_per_batch=seq)
    e, g = _peer_route(h2, peer['wq_t'], peer['k1'], peer['k2'], tb=tm)
    npick = e.shape[0] * e.shape[1]
    tb = min(64, tm)
    y = _peer_gather(e.reshape(npick, t).T, h2, g.reshape(npick, t), x1, gate2, peer['table'],
                     tb=tb, rows_per_batch=seq)
    return y, k, v, kw[:, :dims['idim']], h_re, h_im


def kernel(x_prompt, x_sample, cache_k, cache_v, cache_kidx, state_ssm_re, state_ssm_im, page_table, c_prompt, c_sample, w_ada, b_ada, norm1_g, w_in, q_gain, k_gain, ssm_A_re, ssm_A_im, ssm_B_re, ssm_B_im, ssm_C_re, ssm_C_im, ssm_D, ssm_log_dt, ssm_w_glu, ssm_b_glu, attn_out_g, ssm_out_g, w_out, norm2_g, peer_w_q, peer_sub_keys1, peer_sub_keys2, peer_u, peer_v):
    b, seq, d = x_prompt.shape
    db, dseq, _ = x_sample.shape
    assert dseq == 1
    hd = q_gain.shape[0]
    nkv = cache_k.shape[2]
    idim = cache_kidx.shape[2]
    g, p, c = ssm_B_re.shape
    w = g * c
    a = d - w
    nh = a // hd
    kv = nkv * hd
    ih = (w_in.shape[1] - a - 2 * kv - idim - w) // (idim + 1)
    assert a + 2 * kv + ih * idim + idim + ih + w == w_in.shape[1] and idim + ih <= LANES
    dims = dict(hd=hd, a=a, kv=kv, ih=ih, idim=idim, w=w)

    c_all = jnp.concatenate([c_prompt, c_sample], axis=0)
    pad = (-c_all.shape[0]) % SUBLANES
    c_all = jnp.pad(c_all, ((0, pad), (0, 0)))
    m = _adaln(c_all, w_ada, b_ada)
    mods_p = [mm[:b].reshape(b, 1, d) for mm in jnp.split(m, 6, axis=-1)]
    mods_s = [mm[b:b + db].reshape(1, db, d) for mm in jnp.split(m, 6, axis=-1)]

    cuts = [a, a + kv, a + 2 * kv, a + 2 * kv + ih * idim, a + 2 * kv + ih * idim + idim + ih]
    w_q, w_k, w_v, w_qi, w_kw, w_u = [x.astype(BF16) for x in jnp.split(w_in, cuts, axis=1)]
    w_kw = jnp.pad(w_kw, ((0, 0), (0, LANES - idim - ih)))
    row = lambda x: x.reshape(1, -1)
    wts = dict(norm1_g=row(norm1_g), w_q=w_q, w_k=w_k, w_v=w_v, w_qi=w_qi, w_kw=w_kw, w_u=w_u,
               q_gain=row(q_gain), k_gain=row(k_gain), w_out_a=w_out[:a].astype(BF16),
               w_out_s=w_out[a:].astype(BF16), norm2_g=row(norm2_g))

    gp_shape = jax.ShapeDtypeStruct((g, p), F32)
    cgp_shape = jax.ShapeDtypeStruct((c, g, p), F32)
    ab_re, ab_im, bb_re, bb_im = pl.pallas_call(
        _ssm_param_kernel, out_shape=[gp_shape, gp_shape, cgp_shape, cgp_shape], compiler_params=_cp())(
            ssm_A_re, ssm_A_im, ssm_log_dt.reshape(g, 1), ssm_B_re.transpose(2, 0, 1), ssm_B_im.transpose(2, 0, 1))
    ns = max(1, g // SSM_SLAB_GROUPS)
    ssm = (ab_re.reshape(1, g * p), ab_im.reshape(1, g * p),
           _block_diag(bb_re.transpose(1, 0, 2), ns).astype(BF16), _block_diag(bb_im.transpose(1, 0, 2), ns).astype(BF16),
           _block_diag(ssm_C_re.transpose(0, 2, 1), ns).astype(BF16),
           _block_diag(ssm_C_im.transpose(0, 2, 1), ns).astype(BF16),
           ssm_D.reshape(1, w), ssm_w_glu.astype(BF16), row(ssm_b_glu), row(ssm_out_g))

    hp, nk, half = peer_sub_keys1.shape
    peer = dict(wq_t=peer_w_q.T.astype(BF16), k1=peer_sub_keys1.astype(BF16), k2=peer_sub_keys2.astype(BF16),
                table=_expert_slabs(peer_u, peer_v))
    og = row(attn_out_g)

    attend_p = functools.partial(_prompt_attention, og=og, b=b, nh=nh, nkv=nkv, hd=hd, ih=ih, idim=idim)
    attend_s = functools.partial(_sample_attention, og=og, cache_k=cache_k, cache_v=cache_v, cache_kidx=cache_kidx,
                                 page_table=page_table, nh=nh, nkv=nkv, hd=hd, ih=ih, idim=idim)
    zeros = jnp.zeros((b, g * p), F32)
    yp, k_p, v_p, ki_p, re_p, im_p = _trunk(x_prompt.reshape(b * seq, d), mods_p, wts, ssm, peer, zeros, zeros,
                                            attend_p, nb=b, seq=seq, dims=dims)
    ys, k_s, v_s, ki_s, re_s, im_s = _trunk(x_sample.reshape(db, d), mods_s, wts, ssm, peer,
                                            state_ssm_re.reshape(db, g * p), state_ssm_im.reshape(db, g * p),
                                            attend_s, nb=db, seq=1, dims=dims)
    return (yp.reshape(b, seq, d), ys.reshape(db, 1, d),
            k_p.reshape(b, seq, nkv, hd), v_p.reshape(b, seq, nkv, hd), ki_p.reshape(b, seq, idim),
            re_p.reshape(b, g, p), im_p.reshape(b, g, p),
            k_s.reshape(db, 1, nkv, hd), v_s.reshape(db, 1, nkv, hd), ki_s.reshape(db, 1, idim),
            re_s.reshape(db, g, p), im_s.reshape(db, g, p))
```

```python
import functools

import jax
import jax.numpy as jnp
from jax import lax
from jax.experimental import pallas as pl
from jax.experimental.pallas import tpu as pltpu

F32 = jnp.float32
BF16 = jnp.bfloat16
I32 = jnp.int32

RMS_EPS = 1e-6
TOPK_MAX = 256
PEER_TOPK = 16
LANES = 128
SUBLANES = 8
SSM_SLAB_GROUPS = 16
INT_MIN = -(2 ** 31)
VMEM_LIMIT = 56 * 1024 * 1024


def _cp(*sem, vmem=VMEM_LIMIT, **kw):
    return pltpu.CompilerParams(dimension_semantics=sem, vmem_limit_bytes=vmem, **kw)


def _nt_dot(a, b):
    return lax.dot_general(a, b, (((1,), (1,)), ((), ())), preferred_element_type=F32)


def _dot(a, b):
    return jnp.dot(a, b, preferred_element_type=F32)


def _rms(x, g):
    return x * lax.rsqrt(jnp.mean(x * x, axis=-1, keepdims=True) + RMS_EPS) * g


def _mod_spec(mod, tm, rows_per_batch):
    _, r, d = mod.shape
    if r == 1:
        return pl.BlockSpec((1, 1, d), lambda i, *_: ((i * tm) // rows_per_batch, 0, 0))
    assert r == tm
    return pl.BlockSpec((1, r, d), lambda i, *_: (0, 0, 0))


def _adaln_kernel(c_ref, w_ref, b_ref, o_ref):
    c = c_ref[...]
    s = (c * jax.nn.sigmoid(c)).astype(BF16)
    o_ref[...] = _dot(s, w_ref[...].astype(BF16)) + b_ref[...]


def _adaln(c, w, b):
    m, d = c.shape
    n = w.shape[1]
    tn = 512
    return pl.pallas_call(
        _adaln_kernel, grid=(n // tn,),
        in_specs=[pl.BlockSpec((m, d), lambda j: (0, 0)), pl.BlockSpec((d, tn), lambda j: (0, j)),
                  pl.BlockSpec((1, tn), lambda j: (0, j))],
        out_specs=pl.BlockSpec((m, tn), lambda j: (0, j)),
        out_shape=jax.ShapeDtypeStruct((m, n), F32), compiler_params=_cp("parallel"))(c, w, b.reshape(1, n))


def _head_rms_store(o_ref, p, gain, hd):
    for h in range(p.shape[1] // hd):
        o_ref[:, h * hd:(h + 1) * hd] = _rms(p[:, h * hd:(h + 1) * hd], gain).astype(o_ref.dtype)


def _proj_qkv_kernel(x_ref, sh_ref, sc_ref, g_ref, wq_ref, wk_ref, wv_ref, qg_ref, kg_ref,
                     q_ref, k_ref, v_ref, *, hd):
    hb = (_rms(x_ref[...], g_ref[...]) * (1.0 + sc_ref[0]) + sh_ref[0]).astype(BF16)
    _head_rms_store(q_ref, _dot(hb, wq_ref[...]), qg_ref[...], hd)
    _head_rms_store(k_ref, _dot(hb, wk_ref[...]), kg_ref[...], hd)
    v_ref[...] = _dot(hb, wv_ref[...])


def _proj_idx_kernel(x_ref, sh_ref, sc_ref, g_ref, wqi_ref, wkw_ref, wu_ref, qi_ref, kw_ref, u_ref):
    hb = (_rms(x_ref[...], g_ref[...]) * (1.0 + sc_ref[0]) + sh_ref[0]).astype(BF16)
    qi_ref[...] = _dot(hb, wqi_ref[...]).astype(qi_ref.dtype)
    kw_ref[...] = _dot(hb, wkw_ref[...])
    u_ref[...] = _dot(hb, wu_ref[...])


def _row_call(kernel, tm, rows_per_batch, x, mods, consts, outs):
    t, d = x.shape
    full = lambda a: pl.BlockSpec(a.shape, lambda i: (0,) * a.ndim)
    return pl.pallas_call(
        kernel, grid=(t // tm,),
        in_specs=[pl.BlockSpec((tm, d), lambda i: (i, 0))] + [_mod_spec(m, tm, rows_per_batch) for m in mods]
        + [full(c) for c in consts],
        out_specs=[pl.BlockSpec((tm, n), lambda i: (i, 0)) for n, _ in outs],
        out_shape=[jax.ShapeDtypeStruct((t, n), dt) for n, dt in outs],
        compiler_params=_cp("parallel"))(x, *mods, *consts)


def _ordinal_to_float(o):
    return lax.bitcast_convert_type(jnp.where(o < 0, o ^ 0x7FFFFFFF, o), F32)


def _topk_mask(s, col, k, extra=None, ncols=None):
    m = s.shape[0]
    kf = float(k)
    count = lambda pred: jnp.sum(pred.astype(F32), axis=-1, keepdims=True)

    def bit_step(it, o):
        cand = o + lax.shift_left(jnp.int32(1), 31 - it)
        c = _ordinal_to_float(cand)
        n = count(s >= c)
        if extra is not None:
            n = n + (extra >= c).astype(F32)
        return jnp.where(n >= kf, cand, o)

    v = _ordinal_to_float(lax.fori_loop(0, 32, bit_step, jnp.full((m, 1), INT_MIN, I32)))
    gt = s > v
    eq = s == v
    n_gt = count(gt)
    if extra is not None:
        n_gt = n_gt + (extra > v).astype(F32)
    need = kf - n_gt
    nbits = max(1, (ncols - 1).bit_length())

    def col_step(it, lim):
        cand = lim + lax.shift_left(jnp.int32(1), nbits - 1 - it)
        return jnp.where(count(eq & (col < cand)) < need, cand, lim)

    lim = lax.fori_loop(0, nbits, col_step, jnp.zeros((m, 1), I32))
    finite = s > -jnp.inf
    few = count(finite) + (0.0 if extra is None else 1.0) <= kf
    mask = (few & finite) | (jnp.logical_not(few) & (gt | (eq & (col <= lim))))
    if extra is None:
        return mask, None
    extra_sel = few | (extra > v) | ((extra == v) & (count(eq) < need))
    return mask, extra_sel


def _prompt_attn_kernel(q_ref, qi_ref, kwq_ref, k_ref, v_ref, kwall_ref, og_ref, o_ref, acc_ref,
                        *, nh, nkv, hd, ih, idim, topk, q0):
    tq = q_ref.shape[0]
    s_len = k_ref.shape[1]
    i = pl.program_id(1) + q0
    kib = kwall_ref[0, :, :idim].astype(BF16)
    w = kwq_ref[:, idim:idim + ih]
    qi = qi_ref[...]
    sc = jnp.zeros((tq, s_len), F32)
    for h in range(ih):
        r = _nt_dot(qi[:, h * idim:(h + 1) * idim], kib)
        sc = sc + jnp.maximum(r, 0.0) * w[:, h:h + 1]
    col = lax.broadcasted_iota(I32, (tq, s_len), 1)
    row = lax.broadcasted_iota(I32, (tq, s_len), 0) + i * tq
    adm = col <= row
    sel, _ = _topk_mask(jnp.where(adm, sc, -jnp.inf), col, topk, ncols=s_len)
    kb = k_ref[0].astype(BF16)
    vb = v_ref[0].astype(BF16)
    g = nh // nkv
    scale = hd ** -0.5
    for hk in range(nkv):
        kh = kb[:, hk * hd:(hk + 1) * hd]
        vh = vb[:, hk * hd:(hk + 1) * hd]
        for j in range(g):
            h = hk * g + j
            s = _nt_dot(q_ref[:, h * hd:(h + 1) * hd], kh) * scale
            s = jnp.where(sel, s, -jnp.inf)
            p = jnp.exp(s - jnp.max(s, axis=-1, keepdims=True))
            l = jnp.sum(p, axis=-1, keepdims=True)
            acc_ref[:, h * hd:(h + 1) * hd] = _dot(p.astype(BF16), vh) / l
    o_ref[0] = _rms(acc_ref[...], og_ref[...]).astype(o_ref.dtype)


PROMPT_KEY_CLASSES = 8


def _prompt_attention(q, qi, kw, k, v, og, *, b, nh, nkv, hd, ih, idim):
    t, a = q.shape
    s_len = t // b
    tq = min(128, s_len)
    nq = s_len // tq
    topk = min(TOPK_MAX, s_len // 4)
    ncls = min(PROMPT_KEY_CLASSES, nq)
    per = nq // ncls
    assert per * ncls == nq
    k3, v3, kw3 = (x.reshape(b, s_len, x.shape[1]) for x in (k, v, kw))
    outs = []
    for c in range(ncls):
        q0 = c * per
        sk = (c + 1) * per * tq
        kern = functools.partial(_prompt_attn_kernel, nh=nh, nkv=nkv, hd=hd, ih=ih, idim=idim, topk=topk, q0=q0)
        blk = lambda n, q0=q0: pl.BlockSpec((tq, n), lambda bi, i: (bi * nq + q0 + i, 0))
        keys = lambda n, sk=sk: pl.BlockSpec((1, sk, n), lambda bi, i: (bi, 0, 0))
        outs.append(pl.pallas_call(
            kern, grid=(b, per),
            in_specs=[blk(a), blk(qi.shape[1]), blk(kw.shape[1]), keys(k.shape[1]), keys(v.shape[1]),
                      keys(kw.shape[1]), pl.BlockSpec((1, a), lambda bi, i: (0, 0))],
            out_specs=pl.BlockSpec((1, tq, a), lambda bi, i: (bi, i, 0)),
            out_shape=jax.ShapeDtypeStruct((b, per * tq, a), BF16),
            scratch_shapes=[pltpu.VMEM((tq, a), F32)],
            compiler_params=_cp("parallel", "parallel"))(q, qi, kw, k3, v3, kw3, og))
    return jnp.concatenate(outs, axis=1).reshape(t, a)


def _sample_score_kernel(pt_ref, qi_ref, w_ref, kn_ref, kc_ref, o_ref, sn_ref, buf_ref, sem_ref):
    db, npg = pt_ref.shape
    idim, page = kc_ref.shape[1], kc_ref.shape[2]

    def page_copy(page_id, slot, pg):
        return pltpu.make_async_copy(kc_ref.at[page_id],
                                     buf_ref.at[slot, :, pl.ds(pl.multiple_of(pg * page, page), page)],
                                     sem_ref.at[slot])

    def issue(b, slot):
        def body(pg, c):
            page_copy(pt_ref[b, pg], slot, pg).start()
            return c
        lax.fori_loop(0, npg, body, 0, unroll=8)

    def wait(slot):
        def body(pg, c):
            page_copy(0, slot, pg).wait()
            return c
        lax.fori_loop(0, npg, body, 0, unroll=8)

    issue(0, 0)

    def row(b, c):
        slot = b % 2

        @pl.when(b + 1 < db)
        def _():
            issue(b + 1, 1 - slot)

        wait(slot)
        qi = qi_ref[b]
        w = w_ref[b]
        r = _dot(qi, buf_ref[slot].astype(BF16))
        o_ref[pl.ds(b, 1), :] = jnp.sum(jnp.maximum(r, 0.0) * w, axis=0, keepdims=True)
        rn = _nt_dot(qi, jnp.broadcast_to(kn_ref[b], (LANES, idim)).astype(BF16))
        sn_ref[pl.ds(b, 1), :] = jnp.sum(jnp.maximum(rn, 0.0) * w, axis=0, keepdims=True)
        return c

    lax.fori_loop(0, db, row, 0)


def _sample_select_kernel(sc_ref, sn_ref, m_ref, ns_ref, *, topk):
    sc = sc_ref[...]
    db, past = sc.shape
    col = lax.broadcasted_iota(I32, (db, past), 1)
    mask, new_sel = _topk_mask(sc, col, topk, extra=sn_ref[:, :1], ncols=past)
    m_ref[...] = mask.astype(F32)
    ns_ref[...] = jnp.broadcast_to(new_sel.astype(F32), ns_ref.shape)


def _sample_attn_kernel(pt_ref, q_ref, m_ref, ns_ref, kn_ref, vn_ref, kc_ref, vc_ref, og_ref, o_ref,
                        kbuf_ref, vbuf_ref, sem_ref, mx_ref, l_ref, acc_ref, *, nh, nkv, hd):
    db, npg = pt_ref.shape
    cp = kbuf_ref.shape[1]
    page = kbuf_ref.shape[2] // nkv
    nchunk = npg // cp
    total = db * nchunk
    g = nh // nkv
    scale = hd ** -0.5

    def page_copies(page_id, slot, pg):
        return (pltpu.make_async_copy(kc_ref.at[page_id], kbuf_ref.at[slot, pg], sem_ref.at[0, slot]),
                pltpu.make_async_copy(vc_ref.at[page_id], vbuf_ref.at[slot, pg], sem_ref.at[1, slot]))

    def issue(gi, slot):
        b = gi // nchunk
        c = gi % nchunk
        for pg in range(cp):
            for cpy in page_copies(pt_ref[b, c * cp + pg], slot, pg):
                cpy.start()

    def wait(slot):
        for pg in range(cp):
            for cpy in page_copies(0, slot, pg):
                cpy.wait()

    def per_kv_head(fn, n):
        head_kv = lax.broadcasted_iota(I32, (nh, n), 0) // g
        out = fn(0)
        for hk in range(1, nkv):
            out = jnp.where(head_kv == hk, fn(hk), out)
        return out

    def update(qb, key_of, val_of, n, keep):
        s = per_kv_head(lambda hk: _nt_dot(qb, key_of(hk)), n) * scale
        s = jnp.where(keep, s, -jnp.inf)
        m_old = mx_ref[...]
        m_new = jnp.maximum(m_old, jnp.max(s, axis=-1, keepdims=True))
        m_safe = jnp.where(m_new == -jnp.inf, 0.0, m_new)
        alpha = jnp.exp(m_old - m_safe)
        p = jnp.exp(s - m_safe)
        pb = p.astype(BF16)
        l_ref[...] = alpha * l_ref[...] + jnp.sum(p, axis=-1, keepdims=True)
        acc_ref[...] = alpha * acc_ref[...] + per_kv_head(lambda hk: _dot(pb, val_of(hk)), hd)
        mx_ref[...] = m_new

    issue(0, 0)

    def step(gi, carry):
        slot = gi % 2
        b = gi // nchunk
        c = gi % nchunk

        @pl.when(gi + 1 < total)
        def _():
            issue(gi + 1, 1 - slot)

        wait(slot)

        @pl.when(c == 0)
        def _():
            mx_ref[...] = jnp.full(mx_ref.shape, -jnp.inf, F32)
            l_ref[...] = jnp.zeros(l_ref.shape, F32)
            acc_ref[...] = jnp.zeros(acc_ref.shape, F32)

        qb = q_ref[b].astype(BF16)
        head_rows = lambda buf, hk: buf[slot, :, pl.ds(hk, page, stride=nkv), :].reshape(cp * page, hd).astype(BF16)
        update(qb, lambda hk: head_rows(kbuf_ref, hk), lambda hk: head_rows(vbuf_ref, hk), cp * page,
               m_ref[b, c] > 0.0)

        @pl.when(c == nchunk - 1)
        def _():
            n8 = kn_ref.shape[1]
            first = lax.broadcasted_iota(I32, (1, n8), 1) == 0
            kn = kn_ref[b].astype(BF16)
            vn = vn_ref[b].astype(BF16)
            update(qb, lambda hk: kn[:, hk * hd:(hk + 1) * hd], lambda hk: vn[:, hk * hd:(hk + 1) * hd], n8,
                   first & (ns_ref[pl.ds(b, 1), :1] > 0.0))
            o = acc_ref[...] / l_ref[...]
            ms = jnp.sum(jnp.sum(o * o, axis=-1, keepdims=True), axis=0, keepdims=True) / (nh * hd)
            o_ref[b] = o * lax.rsqrt(ms + RMS_EPS) * og_ref[...]

        return carry

    lax.fori_loop(0, total, step, 0)


SAMPLE_PAGES_PER_CHUNK = 16


def _sample_attention(q, qi, kw, k, v, og, cache_k, cache_v, cache_kidx, page_table, *, nh, nkv, hd, ih, idim):
    db = q.shape[0]
    npg = page_table.shape[1]
    page = cache_k.shape[1]
    past = npg * page
    topk = min(TOPK_MAX, (past + 1) // 4)
    qi3 = qi.reshape(db, ih, idim)
    w3 = kw[:, idim:idim + ih].reshape(db, ih, 1)
    vmem = pl.BlockSpec(memory_space=pltpu.VMEM)
    smem = pl.BlockSpec(memory_space=pltpu.SMEM)
    hbm = pl.BlockSpec(memory_space=pl.ANY)
    scores, s_new = pl.pallas_call(
        _sample_score_kernel,
        in_specs=[smem, vmem, vmem, vmem, hbm], out_specs=[vmem, vmem],
        out_shape=[jax.ShapeDtypeStruct((db, past), F32), jax.ShapeDtypeStruct((db, LANES), F32)],
        scratch_shapes=[pltpu.VMEM((2, idim, past), F32), pltpu.SemaphoreType.DMA((2,))],
        compiler_params=_cp())(page_table, qi3, w3, kw[:, :idim].reshape(db, 1, idim),
                               cache_kidx.transpose(0, 2, 1))
    mask, new_sel = pl.pallas_call(
        functools.partial(_sample_select_kernel, topk=topk),
        out_shape=[jax.ShapeDtypeStruct((db, past), F32), jax.ShapeDtypeStruct((db, LANES), F32)],
        compiler_params=_cp())(scores, s_new)
    cp = min(SAMPLE_PAGES_PER_CHUNK, npg)
    assert npg % cp == 0
    rep = lambda a: jnp.broadcast_to(a[:, None, :], (db, SUBLANES, a.shape[1]))
    out = pl.pallas_call(
        functools.partial(_sample_attn_kernel, nh=nh, nkv=nkv, hd=hd),
        in_specs=[smem, vmem, vmem, vmem, vmem, vmem, hbm, hbm, vmem], out_specs=vmem,
        out_shape=jax.ShapeDtypeStruct((db, nh, hd), F32),
        scratch_shapes=[pltpu.VMEM((2, cp, page * nkv, hd), F32), pltpu.VMEM((2, cp, page * nkv, hd), F32),
                        pltpu.SemaphoreType.DMA((2, 2)),
                        pltpu.VMEM((nh, 1), F32), pltpu.VMEM((nh, 1), F32), pltpu.VMEM((nh, hd), F32)],
        compiler_params=_cp())(
            page_table, q.astype(F32).reshape(db, nh, hd), mask.reshape(db, npg // cp, 1, cp * page), new_sel,
            rep(k), rep(v), cache_k.reshape(-1, page * nkv, hd), cache_v.reshape(-1, page * nkv, hd),
            og.reshape(nh, hd))
    return out.reshape(db, nh * hd).astype(BF16)


def _ssm_param_kernel(are_ref, aim_ref, ldt_ref, bre_ref, bim_ref, abre_ref, abim_ref, bbre_ref, bbim_ref):
    a_re = are_ref[...]
    a_im = aim_ref[...]
    dt = jnp.exp(ldt_ref[...])
    mag = jnp.exp(a_re * dt)
    ab_re = mag * jnp.cos(a_im * dt)
    ab_im = mag * jnp.sin(a_im * dt)
    den = a_re * a_re + a_im * a_im
    nr = ab_re - 1.0
    co_re = (nr * a_re + ab_im * a_im) / den
    co_im = (ab_im * a_re - nr * a_im) / den
    abre_ref[...] = ab_re
    abim_ref[...] = ab_im
    b_re = bre_ref[...]
    b_im = bim_ref[...]
    bbre_ref[...] = co_re[None] * b_re - co_im[None] * b_im
    bbim_ref[...] = co_re[None] * b_im + co_im[None] * b_re


def _s5_kernel(u_ref, h0re_ref, h0im_ref, ar_ref, ai_ref, wre_ref, wim_ref, cre_ref, cim_ref, d_ref,
               wglu_ref, bglu_ref, og_ref, s_ref, hre_ref, him_ref, bure_ref, buim_ref, y_ref, *, r, tc):
    ci = pl.program_id(0)
    ns, sc_w, sp_w = wre_ref.shape

    @pl.when(ci == 0)
    def _():
        hre_ref[...] = h0re_ref[...]
        him_ref[...] = h0im_ref[...]

    for s in range(ns):
        cols = slice(s * sc_w, (s + 1) * sc_w)
        st = slice(s * sp_w, (s + 1) * sp_w)
        ub = u_ref[:, cols].astype(BF16)
        bure_ref[...] = _dot(ub, wre_ref[s])
        buim_ref[...] = _dot(ub, wim_ref[s])
        a_r = jnp.broadcast_to(ar_ref[:, st], (r, sp_w))
        a_i = jnp.broadcast_to(ai_ref[:, st], (r, sp_w))

        def step(t, carry):
            hr, hi = carry
            rows = pl.ds(pl.multiple_of(t * r, r), r)
            nr = a_r * hr - a_i * hi + bure_ref[rows, :]
            ni = a_r * hi + a_i * hr + buim_ref[rows, :]
            bure_ref[rows, :] = nr
            buim_ref[rows, :] = ni
            return nr, ni

        hr, hi = lax.fori_loop(0, tc, step, (hre_ref[:, st], him_ref[:, st]))
        hre_ref[:, st] = hr
        him_ref[:, st] = hi
        y_ref[:, cols] = (_dot(bure_ref[...].astype(BF16), cre_ref[s])
                          - _dot(buim_ref[...].astype(BF16), cim_ref[s]))
    y = jax.nn.gelu(y_ref[...] + d_ref[...] * u_ref[...])
    z = _dot(y.astype(BF16), wglu_ref[...]) + bglu_ref[...]
    s_ref[...] = _rms(y * jax.nn.sigmoid(z), og_ref[...]).astype(s_ref.dtype)


def _s5(u_tb, h0_re, h0_im, ab_re, ab_im, w_re, w_im, c_re, c_im, d, w_glu, b_glu, og, *, r, tc):
    rows, w = u_tb.shape
    gp = h0_re.shape[1]
    full = lambda a: pl.BlockSpec(a.shape, lambda i: (0,) * a.ndim)
    consts = (h0_re, h0_im, ab_re, ab_im, w_re, w_im, c_re, c_im, d, w_glu, b_glu, og)
    sp_w = w_re.shape[2]
    return pl.pallas_call(
        functools.partial(_s5_kernel, r=r, tc=tc), grid=(rows // (r * tc),),
        in_specs=[pl.BlockSpec((r * tc, w), lambda i: (i, 0))] + [full(c) for c in consts],
        out_specs=[pl.BlockSpec((r * tc, w), lambda i: (i, 0)), full(h0_re), full(h0_im)],
        out_shape=[jax.ShapeDtypeStruct((rows, w), BF16), jax.ShapeDtypeStruct((r, gp), F32),
                   jax.ShapeDtypeStruct((r, gp), F32)],
        scratch_shapes=[pltpu.VMEM((r * tc, sp_w), F32), pltpu.VMEM((r * tc, sp_w), F32),
                        pltpu.VMEM((r * tc, w), F32)],
        compiler_params=_cp("arbitrary"))(u_tb, *consts)


def _out_proj_kernel(x_ref, g1_ref, sh_ref, sc_ref, a_ref, s_ref, wa_ref, ws_ref, n2_ref, x1_ref, h2_ref):
    mix = _dot(a_ref[...], wa_ref[...]) + _dot(s_ref[...], ws_ref[...])
    x1 = x_ref[...] + g1_ref[0] * mix
    x1_ref[...] = x1
    h2_ref[...] = _rms(x1, n2_ref[...]) * (1.0 + sc_ref[0]) + sh_ref[0]


def _out_proj(x, gate1, shift2, scale2, a_n, s_n, w_a, w_s, n2, *, tm, rows_per_batch):
    t, d = x.shape
    full = lambda a: pl.BlockSpec(a.shape, lambda i: (0,) * a.ndim)
    rowblk = lambda n: pl.BlockSpec((tm, n), lambda i: (i, 0))
    return pl.pallas_call(
        _out_proj_kernel, grid=(t // tm,),
        in_specs=[rowblk(d)] + [_mod_spec(m, tm, rows_per_batch) for m in (gate1, shift2, scale2)]
        + [rowblk(a_n.shape[1]), rowblk(s_n.shape[1]), full(w_a), full(w_s), full(n2)],
        out_specs=[rowblk(d), rowblk(d)],
        out_shape=[jax.ShapeDtypeStruct((t, d), F32), jax.ShapeDtypeStruct((t, d), F32)],
        compiler_params=_cp("parallel"))(x, gate1, shift2, scale2, a_n, s_n, w_a, w_s, n2)


def _take_top(s, order, k, pay=None):
    big = 3.0e38
    kid = lax.broadcasted_iota(I32, (k, s.shape[1]), 0)
    vals = jnp.zeros((k, s.shape[1]), F32)
    picks = jnp.zeros((k, s.shape[1]), F32)
    for it in range(k):
        m = jnp.max(s, axis=0, keepdims=True)
        first = jnp.min(jnp.where(s == m, order, big), axis=0, keepdims=True)
        hit = order == first
        pv = first if pay is None else jnp.max(jnp.where(hit, pay, -1.0), axis=0, keepdims=True)
        vals = jnp.where(kid == it, m, vals)
        picks = jnp.where(kid == it, pv, picks)
        s = jnp.where(hit, -jnp.inf, s)
    return vals, picks


def _pair_candidates(v1, i1, v2, i2, nk):
    k, tb = v1.shape
    assert k == 16
    widths = [k] + [SUBLANES] * (k // 2 - 1)
    row = lambda w: lax.broadcasted_iota(I32, (w, tb), 0).astype(F32)
    sums = [v1[a:a + 1, :] + v2[:w, :] for a, w in enumerate(widths)]
    flat = [row(w) + float(a * k) for a, w in enumerate(widths)]
    ids = [i1[a:a + 1, :] * float(nk) + i2[:w, :] for a, w in enumerate(widths)]
    sums.append(v1[k // 2:, :] + v2[:1, :])
    flat.append((row(k // 2) + float(k // 2)) * float(k))
    ids.append(i1[k // 2:, :] * float(nk) + i2[:1, :])
    return jnp.concatenate(sums, axis=0), jnp.concatenate(flat, axis=0), jnp.concatenate(ids, axis=0)


def _peer_route_kernel(h_ref, wq_ref, k1_ref, k2_ref, e_ref, g_ref, *, nk):
    half = k1_ref.shape[2]
    tb = h_ref.shape[0]
    k = PEER_TOPK
    qt = _nt_dot(wq_ref[...], h_ref[...].astype(BF16))
    s1 = _dot(k1_ref[0], qt[:half, :].astype(BF16))
    s2 = _dot(k2_ref[0], qt[half:, :].astype(BF16))
    rid = lax.broadcasted_iota(I32, (nk, tb), 0).astype(F32)
    v1, i1 = _take_top(s1, rid, k)
    v2, i2 = _take_top(s2, rid, k)
    cand, flat, ids = _pair_candidates(v1, i1, v2, i2, nk)
    sc, e = _take_top(cand, flat, k, pay=ids)
    p = jnp.exp(sc - jnp.max(sc, axis=0, keepdims=True))
    g_ref[0] = p / jnp.sum(p, axis=0, keepdims=True)
    e_ref[0] = e.astype(I32)


def _peer_route(h2, wq_t, k1, k2, *, tb):
    t, d = h2.shape
    hp, nk, half = k1.shape
    return pl.pallas_call(
        functools.partial(_peer_route_kernel, nk=nk), grid=(t // tb, hp),
        in_specs=[pl.BlockSpec((tb, d), lambda i, h: (i, 0)), pl.BlockSpec((2 * half, d), lambda i, h: (h, 0)),
                  pl.BlockSpec((1, nk, half), lambda i, h: (h, 0, 0)),
                  pl.BlockSpec((1, nk, half), lambda i, h: (h, 0, 0))],
        out_specs=[pl.BlockSpec((1, PEER_TOPK, tb), lambda i, h: (h, 0, i)),
                   pl.BlockSpec((1, PEER_TOPK, tb), lambda i, h: (h, 0, i))],
        out_shape=[jax.ShapeDtypeStruct((hp, PEER_TOPK, t), I32), jax.ShapeDtypeStruct((hp, PEER_TOPK, t), F32)],
        compiler_params=_cp("parallel", "arbitrary"))(h2, wq_t, k1, k2)


PEER_GATHER_BATCH = 2
PEER_RING_GROUPS = 4
PEER_RING_AHEAD = PEER_RING_GROUPS - 1


def _peer_gather_kernel(e_ref, en_ref, h_ref, g_ref, x1_ref, gate_ref, tab_ref, y_ref, *scratch):
    *bufs, wgt_ref, sem_ref = scratch
    nbt = PEER_GATHER_BATCH
    ngrp = len(bufs) // nbt
    ahead = PEER_RING_AHEAD
    tb = h_ref.shape[0]
    npick = bufs[0].shape[0]
    R = bufs[0].shape[1] // 2
    i = pl.program_id(0)

    def slab_copy(e, slot, p):
        return pltpu.make_async_copy(tab_ref.at[e], bufs[slot].at[p], sem_ref.at[slot])

    def issue(idx_ref, j, slot):
        for p in range(npick):
            slab_copy(idx_ref[j, p], slot, p).start(priority=p % 2)

    def wait(slot):
        pltpu.make_async_copy(tab_ref.at[pl.ds(0, npick)], bufs[slot], sem_ref.at[slot]).wait()

    group_sum = (lax.broadcasted_iota(I32, (npick, npick * R), 1) // R
                 == lax.broadcasted_iota(I32, (npick, npick * R), 0)).astype(BF16)
    gw = g_ref.shape[1]
    g_lane = lax.broadcasted_iota(I32, (npick, gw), 1)

    def compute(j, slot):
        buf = bufs[slot]
        xb = h_ref[j]
        prods = [buf[p, :R, :] * xb for p in range(npick)]
        hid = jnp.sum(_dot(group_sum, jnp.concatenate(prods, axis=0)), axis=-1, keepdims=True)
        jcol = (i * tb) % gw + j
        gcol = jnp.sum(jnp.where(g_lane == jcol, g_ref[...], 0.0), axis=-1, keepdims=True)
        wgt_ref[slot % nbt] = jnp.broadcast_to(gcol * jax.nn.gelu(hid), (npick, LANES))
        nacc = 4
        acc = [jnp.zeros((R, LANES), F32) for _ in range(nacc)]
        for p in range(npick):
            w_p = jnp.broadcast_to(wgt_ref[slot % nbt, p:p + 1, :], (R, LANES))
            acc[p % nacc] = acc[p % nacc] + w_p * buf[p, R:, :].astype(F32)
        out = sum(acc[1:], acc[0])
        gate = gate_ref[0, 0] if gate_ref.shape[1] == 1 else gate_ref[0, j]
        y_ref[j] = x1_ref[j] + gate * out

    @pl.when(i == 0)
    def _():
        for s in range(ahead * nbt):
            issue(e_ref, s, s)

    def ring_turn(k, c):
        for grp in range(ngrp):
            j0 = (ngrp * k + grp) * nbt
            for s in range(nbt):
                wait(grp * nbt + s)
            for s in range(nbt):
                issue(en_ref, j0 + s, ((grp + ahead) % ngrp) * nbt + s)
            for s in range(nbt):
                compute(j0 + s, grp * nbt + s)
        return c

    lax.fori_loop(0, tb // (ngrp * nbt), ring_turn, 0)

    @pl.when(i == pl.num_programs(0) - 1)
    def _():
        for s in range(ahead * nbt):
            wait(s)


def _peer_gather(e_tok, h2, g_t, x1, gate2, table, *, tb, rows_per_batch):
    t, d = h2.shape
    npick = e_tok.shape[1]
    rows = table.shape[1] // 2
    assert rows * LANES == d
    nbt = PEER_GATHER_BATCH
    nbuf = PEER_RING_GROUPS * nbt
    assert tb % nbuf == 0 and t % tb == 0
    slab = lambda a: a.reshape(*a.shape[:-1], rows, LANES)
    gate = slab(gate2)
    if gate.shape[1] == 1:
        gate_spec = pl.BlockSpec((1, 1, rows, LANES), lambda i: ((i * tb) // rows_per_batch, 0, 0, 0))
    else:
        assert gate.shape[1] == tb
        gate_spec = pl.BlockSpec((1, tb, rows, LANES), lambda i: (0, 0, 0, 0))
    tokblk = pl.BlockSpec((tb, rows, LANES), lambda i: (i, 0, 0))
    idxblk = pl.BlockSpec((tb, npick), lambda i: (i, 0), memory_space=pltpu.SMEM)
    gw = t if t < LANES else max(tb, LANES)
    assert gw % tb == 0 and t % gw == 0
    y = pl.pallas_call(
        _peer_gather_kernel, grid=(t // tb,),
        in_specs=[idxblk, idxblk, tokblk, pl.BlockSpec((npick, gw), lambda i: (0, (i * tb) // gw)), tokblk,
                  gate_spec, pl.BlockSpec(memory_space=pl.ANY)],
        out_specs=tokblk, out_shape=jax.ShapeDtypeStruct((t, rows, LANES), F32),
        scratch_shapes=[pltpu.VMEM((npick, 2 * rows, LANES), BF16)] * nbuf
        + [pltpu.VMEM((nbt, npick, LANES), F32), pltpu.SemaphoreType.DMA((nbuf,))],
        compiler_params=_cp("arbitrary", disable_bounds_checks=True))(
            e_tok, jnp.roll(e_tok, -PEER_RING_AHEAD * nbt, axis=0), slab(h2.astype(BF16)), g_t, slab(x1), gate, table)
    return y.reshape(t, d)


def _expert_slabs(u, v):
    e, d = u.shape
    r = d // LANES
    assert r * LANES == d and r % (2 * SUBLANES) == 0
    return jnp.concatenate([u.astype(BF16).reshape(e, r, LANES), v.astype(BF16).reshape(e, r, LANES)], axis=1)


def _block_diag(m, ns):
    g, a, b = m.shape
    gs = g // ns
    eye = jnp.eye(gs, dtype=m.dtype)
    return jnp.einsum('sgab,gh->sgahb', m.reshape(ns, gs, a, b), eye).reshape(ns, gs * a, gs * b)


def _trunk(x, mods, wts, ssm, peer, h0_re, h0_im, attend, *, nb, seq, dims):
    t, d = x.shape
    shift1, scale1, gate1, shift2, scale2, gate2 = mods
    per_batch_mod = shift1.shape[1] == 1
    tm = min(256, seq) if per_batch_mod else t
    big_tm = min(512, seq) if per_batch_mod else t
    q, k, v = _row_call(functools.partial(_proj_qkv_kernel, hd=dims['hd']), big_tm, seq, x, (shift1, scale1),
                        (wts['norm1_g'], wts['w_q'], wts['w_k'], wts['w_v'], wts['q_gain'], wts['k_gain']),
                        ((dims['a'], BF16), (dims['kv'], F32), (dims['kv'], F32)))
    qi, kw, u = _row_call(_proj_idx_kernel, big_tm, seq, x, (shift1, scale1),
                          (wts['norm1_g'], wts['w_qi'], wts['w_kw'], wts['w_u']),
                          ((dims['ih'] * dims['idim'], BF16), (LANES, F32), (dims['w'], F32)))
    a_n = attend(q, qi, kw, k, v)
    w = dims['w']
    u_tb = u.reshape(nb, seq, w).transpose(1, 0, 2).reshape(t, w)
    tc = min(64, seq)
    s_tb, h_re, h_im = _s5(u_tb, h0_re, h0_im, *ssm, r=nb, tc=tc)
    s_n = s_tb.reshape(seq, nb, w).transpose(1, 0, 2).reshape(t, w)
    x1, h2 = _out_proj(x, gate1, shift2, scale2, a_n, s_n, wts['w_out_a'], wts['w_out_s'], wts['norm2_g'],
                       tm=tm, rows_per_batch=seq)
    e, g = _peer_route(h2, peer['wq_t'], peer['k1'], peer['k2'], tb=tm)
    npick = e.shape[0] * e.shape[1]
    tb = min(64, tm)
    y = _peer_gather(e.reshape(npick, t).T, h2, g.reshape(npick, t), x1, gate2, peer['table'],
                     tb=tb, rows_per_batch=seq)
    return y, k, v, kw[:, :dims['idim']], h_re, h_im


def kernel(x_prompt, x_sample, cache_k, cache_v, cache_kidx, state_ssm_re, state_ssm_im, page_table, c_prompt, c_sample, w_ada, b_ada, norm1_g, w_in, q_gain, k_gain, ssm_A_re, ssm_A_im, ssm_B_re, ssm_B_im, ssm_C_re, ssm_C_im, ssm_D, ssm_log_dt, ssm_w_glu, ssm_b_glu, attn_out_g, ssm_out_g, w_out, norm2_g, peer_w_q, peer_sub_keys1, peer_sub_keys2, peer_u, peer_v):
    b, seq, d = x_prompt.shape
    db, dseq, _ = x_sample.shape
    assert dseq == 1
    hd = q_gain.shape[0]
    nkv = cache_k.shape[2]
    idim = cache_kidx.shape[2]
    g, p, c = ssm_B_re.shape
    w = g * c
    a = d - w
    nh = a // hd
    kv = nkv * hd
    ih = (w_in.shape[1] - a - 2 * kv - idim - w) // (idim + 1)
    assert a + 2 * kv + ih * idim + idim + ih + w == w_in.shape[1] and idim + ih <= LANES
    dims = dict(hd=hd, a=a, kv=kv, ih=ih, idim=idim, w=w)

    c_all = jnp.concatenate([c_prompt, c_sample], axis=0)
    pad = (-c_all.shape[0]) % SUBLANES
    c_all = jnp.pad(c_all, ((0, pad), (0, 0)))
    m = _adaln(c_all, w_ada, b_ada)
    mods_p = [mm[:b].reshape(b, 1, d) for mm in jnp.split(m, 6, axis=-1)]
    mods_s = [mm[b:b + db].reshape(1, db, d) for mm in jnp.split(m, 6, axis=-1)]

    cuts = [a, a + kv, a + 2 * kv, a + 2 * kv + ih * idim, a + 2 * kv + ih * idim + idim + ih]
    w_q, w_k, w_v, w_qi, w_kw, w_u = [x.astype(BF16) for x in jnp.split(w_in, cuts, axis=1)]
    w_kw = jnp.pad(w_kw, ((0, 0), (0, LANES - idim - ih)))
    row = lambda x: x.reshape(1, -1)
    wts = dict(norm1_g=row(norm1_g), w_q=w_q, w_k=w_k, w_v=w_v, w_qi=w_qi, w_kw=w_kw, w_u=w_u,
               q_gain=row(q_gain), k_gain=row(k_gain), w_out_a=w_out[:a].astype(BF16),
               w_out_s=w_out[a:].astype(BF16), norm2_g=row(norm2_g))

    gp_shape = jax.ShapeDtypeStruct((g, p), F32)
    cgp_shape = jax.ShapeDtypeStruct((c, g, p), F32)
    ab_re, ab_im, bb_re, bb_im = pl.pallas_call(
        _ssm_param_kernel, out_shape=[gp_shape, gp_shape, cgp_shape, cgp_shape], compiler_params=_cp())(
            ssm_A_re, ssm_A_im, ssm_log_dt.reshape(g, 1), ssm_B_re.transpose(2, 0, 1), ssm_B_im.transpose(2, 0, 1))
    ns = max(1, g // SSM_SLAB_GROUPS)
    ssm = (ab_re.reshape(1, g * p), ab_im.reshape(1, g * p),
           _block_diag(bb_re.transpose(1, 0, 2), ns).astype(BF16), _block_diag(bb_im.transpose(1, 0, 2), ns).astype(BF16),
           _block_diag(ssm_C_re.transpose(0, 2, 1), ns).astype(BF16),
           _block_diag(ssm_C_im.transpose(0, 2, 1), ns).astype(BF16),
           ssm_D.reshape(1, w), ssm_w_glu.astype(BF16), row(ssm_b_glu), row(ssm_out_g))

    hp, nk, half = peer_sub_keys1.shape
    peer = dict(wq_t=peer_w_q.T.astype(BF16), k1=peer_sub_keys1.astype(BF16), k2=peer_sub_keys2.astype(BF16),
                table=_expert_slabs(peer_u, peer_v))
    og = row(attn_out_g)

    attend_p = functools.partial(_prompt_attention, og=og, b=b, nh=nh, nkv=nkv, hd=hd, ih=ih, idim=idim)
    attend_s = functools.partial(_sample_attention, og=og, cache_k=cache_k, cache_v=cache_v, cache_kidx=cache_kidx,
                                 page_table=page_table, nh=nh, nkv=nkv, hd=hd, ih=ih, idim=idim)
    zeros = jnp.zeros((b, g * p), F32)
    yp, k_p, v_p, ki_p, re_p, im_p = _trunk(x_prompt.reshape(b * seq, d), mods_p, wts, ssm, peer, zeros, zeros,
                                            attend_p, nb=b, seq=seq, dims=dims)
    ys, k_s, v_s, ki_s, re_s, im_s = _trunk(x_sample.reshape(db, d), mods_s, wts, ssm, peer,
                                            state_ssm_re.reshape(db, g * p), state_ssm_im.reshape(db, g * p),
                                            attend_s, nb=db, seq=1, dims=dims)
    return (yp.reshape(b, seq, d), ys.reshape(db, 1, d),
            k_p.reshape(b, seq, nkv, hd), v_p.reshape(b, seq, nkv, hd), ki_p.reshape(b, seq, idim),
            re_p.reshape(b, g, p), im_p.reshape(b, g, p),
            k_s.reshape(db, 1, nkv, hd), v_s.reshape(db, 1, nkv, hd), ki_s.reshape(db, 1, idim),
            re_s.reshape(db, g, p), im_s.reshape(db, g, p))
```

```python
import functools

import jax
import jax.numpy as jnp
from jax import lax
from jax.experimental import pallas as pl
from jax.experimental.pallas import tpu as pltpu

F32 = jnp.float32
BF16 = jnp.bfloat16
I32 = jnp.int32

RMS_EPS = 1e-6
TOPK_MAX = 256
PEER_TOPK = 16
LANES = 128
SUBLANES = 8
SSM_SLAB_GROUPS = 16
INT_MIN = -(2 ** 31)
VMEM_LIMIT = 56 * 1024 * 1024


def _cp(*sem, vmem=VMEM_LIMIT, **kw):
    return pltpu.CompilerParams(dimension_semantics=sem, vmem_limit_bytes=vmem, **kw)


def _nt_dot(a, b):
    return lax.dot_general(a, b, (((1,), (1,)), ((), ())), preferred_element_type=F32)


def _dot(a, b):
    return jnp.dot(a, b, preferred_element_type=F32)


def _rms(x, g):
    return x * lax.rsqrt(jnp.mean(x * x, axis=-1, keepdims=True) + RMS_EPS) * g


def _mod_spec(mod, tm, rows_per_batch):
    _, r, d = mod.shape
    if r == 1:
        return pl.BlockSpec((1, 1, d), lambda i, *_: ((i * tm) // rows_per_batch, 0, 0))
    assert r == tm
    return pl.BlockSpec((1, r, d), lambda i, *_: (0, 0, 0))


def _adaln_kernel(c_ref, w_ref, b_ref, o_ref):
    c = c_ref[...]
    s = (c * jax.nn.sigmoid(c)).astype(BF16)
    o_ref[...] = _dot(s, w_ref[...].astype(BF16)) + b_ref[...]


def _adaln(c, w, b):
    m, d = c.shape
    n = w.shape[1]
    tn = 512
    return pl.pallas_call(
        _adaln_kernel, grid=(n // tn,),
        in_specs=[pl.BlockSpec((m, d), lambda j: (0, 0)), pl.BlockSpec((d, tn), lambda j: (0, j)),
                  pl.BlockSpec((1, tn), lambda j: (0, j))],
        out_specs=pl.BlockSpec((m, tn), lambda j: (0, j)),
        out_shape=jax.ShapeDtypeStruct((m, n), F32), compiler_params=_cp("parallel"))(c, w, b.reshape(1, n))


def _head_rms_store(o_ref, p, gain, hd):
    for h in range(p.shape[1] // hd):
        o_ref[:, h * hd:(h + 1) * hd] = _rms(p[:, h * hd:(h + 1) * hd], gain).astype(o_ref.dtype)


def _proj_qkv_kernel(x_ref, sh_ref, sc_ref, g_ref, wq_ref, wk_ref, wv_ref, qg_ref, kg_ref,
                     q_ref, k_ref, v_ref, *, hd):
    hb = (_rms(x_ref[...], g_ref[...]) * (1.0 + sc_ref[0]) + sh_ref[0]).astype(BF16)
    _head_rms_store(q_ref, _dot(hb, wq_ref[...]), qg_ref[...], hd)
    _head_rms_store(k_ref, _dot(hb, wk_ref[...]), kg_ref[...], hd)
    v_ref[...] = _dot(hb, wv_ref[...])


def _proj_idx_kernel(x_ref, sh_ref, sc_ref, g_ref, wqi_ref, wkw_ref, wu_ref, qi_ref, kw_ref, u_ref):
    hb = (_rms(x_ref[...], g_ref[...]) * (1.0 + sc_ref[0]) + sh_ref[0]).astype(BF16)
    qi_ref[...] = _dot(hb, wqi_ref[...]).astype(qi_ref.dtype)
    kw_ref[...] = _dot(hb, wkw_ref[...])
    u_ref[...] = _dot(hb, wu_ref[...])


def _row_call(kernel, tm, rows_per_batch, x, mods, consts, outs):
    t, d = x.shape
    full = lambda a: pl.BlockSpec(a.shape, lambda i: (0,) * a.ndim)
    return pl.pallas_call(
        kernel, grid=(t // tm,),
        in_specs=[pl.BlockSpec((tm, d), lambda i: (i, 0))] + [_mod_spec(m, tm, rows_per_batch) for m in mods]
        + [full(c) for c in consts],
        out_specs=[pl.BlockSpec((tm, n), lambda i: (i, 0)) for n, _ in outs],
        out_shape=[jax.ShapeDtypeStruct((t, n), dt) for n, dt in outs],
        compiler_params=_cp("parallel"))(x, *mods, *consts)


def _ordinal_to_float(o):
    return lax.bitcast_convert_type(jnp.where(o < 0, o ^ 0x7FFFFFFF, o), F32)


def _topk_mask(s, col, k, extra=None, ncols=None):
    m = s.shape[0]
    kf = float(k)
    count = lambda pred: jnp.sum(pred.astype(F32), axis=-1, keepdims=True)

    def bit_step(it, o):
        cand = o + lax.shift_left(jnp.int32(1), 31 - it)
        c = _ordinal_to_float(cand)
        n = count(s >= c)
        if extra is not None:
            n = n + (extra >= c).astype(F32)
        return jnp.where(n >= kf, cand, o)

    v = _ordinal_to_float(lax.fori_loop(0, 32, bit_step, jnp.full((m, 1), INT_MIN, I32)))
    gt = s > v
    eq = s == v
    n_gt = count(gt)
    if extra is not None:
        n_gt = n_gt + (extra > v).astype(F32)
    need = kf - n_gt
    nbits = max(1, (ncols - 1).bit_length())

    def col_step(it, lim):
        cand = lim + lax.shift_left(jnp.int32(1), nbits - 1 - it)
        return jnp.where(count(eq & (col < cand)) < need, cand, lim)

    lim = lax.fori_loop(0, nbits, col_step, jnp.zeros((m, 1), I32))
    finite = s > -jnp.inf
    few = count(finite) + (0.0 if extra is None else 1.0) <= kf
    mask = (few & finite) | (jnp.logical_not(few) & (gt | (eq & (col <= lim))))
    if extra is None:
        return mask, None
    extra_sel = few | (extra > v) | ((extra == v) & (count(eq) < need))
    return mask, extra_sel


def _prompt_attn_kernel(q_ref, qi_ref, kwq_ref, k_ref, v_ref, kwall_ref, og_ref, o_ref, acc_ref,
                        *, nh, nkv, hd, ih, idim, topk, q0):
    tq = q_ref.shape[0]
    s_len = k_ref.shape[1]
    i = pl.program_id(1) + q0
    kib = kwall_ref[0, :, :idim].astype(BF16)
    w = kwq_ref[:, idim:idim + ih]
    qi = qi_ref[...]
    sc = jnp.zeros((tq, s_len), F32)
    for h in range(ih):
        r = _nt_dot(qi[:, h * idim:(h + 1) * idim], kib)
        sc = sc + jnp.maximum(r, 0.0) * w[:, h:h + 1]
    col = lax.broadcasted_iota(I32, (tq, s_len), 1)
    row = lax.broadcasted_iota(I32, (tq, s_len), 0) + i * tq
    adm = col <= row
    sel, _ = _topk_mask(jnp.where(adm, sc, -jnp.inf), col, topk, ncols=s_len)
    kb = k_ref[0].astype(BF16)
    vb = v_ref[0].astype(BF16)
    g = nh // nkv
    scale = hd ** -0.5
    for hk in range(nkv):
        kh = kb[:, hk * hd:(hk + 1) * hd]
        vh = vb[:, hk * hd:(hk + 1) * hd]
        for j in range(g):
            h = hk * g + j
            s = _nt_dot(q_ref[:, h * hd:(h + 1) * hd], kh) * scale
            s = jnp.where(sel, s, -jnp.inf)
            p = jnp.exp(s - jnp.max(s, axis=-1, keepdims=True))
            l = jnp.sum(p, axis=-1, keepdims=True)
            acc_ref[:, h * hd:(h + 1) * hd] = _dot(p.astype(BF16), vh) / l
    o_ref[0] = _rms(acc_ref[...], og_ref[...]).astype(o_ref.dtype)


PROMPT_KEY_CLASSES = 8


def _prompt_attention(q, qi, kw, k, v, og, *, b, nh, nkv, hd, ih, idim):
    t, a = q.shape
    s_len = t // b
    tq = min(128, s_len)
    nq = s_len // tq
    topk = min(TOPK_MAX, s_len // 4)
    ncls = min(PROMPT_KEY_CLASSES, nq)
    per = nq // ncls
    assert per * ncls == nq
    k3, v3, kw3 = (x.reshape(b, s_len, x.shape[1]) for x in (k, v, kw))
    outs = []
    for c in range(ncls):
        q0 = c * per
        sk = (c + 1) * per * tq
        kern = functools.partial(_prompt_attn_kernel, nh=nh, nkv=nkv, hd=hd, ih=ih, idim=idim, topk=topk, q0=q0)
        blk = lambda n, q0=q0: pl.BlockSpec((tq, n), lambda bi, i: (bi * nq + q0 + i, 0))
        keys = lambda n, sk=sk: pl.BlockSpec((1, sk, n), lambda bi, i: (bi, 0, 0))
        outs.append(pl.pallas_call(
            kern, grid=(b, per),
            in_specs=[blk(a), blk(qi.shape[1]), blk(kw.shape[1]), keys(k.shape[1]), keys(v.shape[1]),
                      keys(kw.shape[1]), pl.BlockSpec((1, a), lambda bi, i: (0, 0))],
            out_specs=pl.BlockSpec((1, tq, a), lambda bi, i: (bi, i, 0)),
            out_shape=jax.ShapeDtypeStruct((b, per * tq, a), BF16),
            scratch_shapes=[pltpu.VMEM((tq, a), F32)],
            compiler_params=_cp("parallel", "parallel"))(q, qi, kw, k3, v3, kw3, og))
    return jnp.concatenate(outs, axis=1).reshape(t, a)


def _sample_score_kernel(pt_ref, qi_ref, w_ref, kn_ref, kc_ref, o_ref, sn_ref, buf_ref, sem_ref):
    db, npg = pt_ref.shape
    idim, page = kc_ref.shape[1], kc_ref.shape[2]

    def page_copy(page_id, slot, pg):
        return pltpu.make_async_copy(kc_ref.at[page_id],
                                     buf_ref.at[slot, :, pl.ds(pl.multiple_of(pg * page, page), page)],
                                     sem_ref.at[slot])

    def issue(b, slot):
        def body(pg, c):
            page_copy(pt_ref[b, pg], slot, pg).start()
            return c
        lax.fori_loop(0, npg, body, 0, unroll=8)

    def wait(slot):
        def body(pg, c):
            page_copy(0, slot, pg).wait()
            return c
        lax.fori_loop(0, npg, body, 0, unroll=8)

    issue(0, 0)

    def row(b, c):
        slot = b % 2

        @pl.when(b + 1 < db)
        def _():
            issue(b + 1, 1 - slot)

        wait(slot)
        qi = qi_ref[b]
        w = w_ref[b]
        r = _dot(qi, buf_ref[slot].astype(BF16))
        o_ref[pl.ds(b, 1), :] = jnp.sum(jnp.maximum(r, 0.0) * w, axis=0, keepdims=True)
        rn = _nt_dot(qi, jnp.broadcast_to(kn_ref[b], (LANES, idim)).astype(BF16))
        sn_ref[pl.ds(b, 1), :] = jnp.sum(jnp.maximum(rn, 0.0) * w, axis=0, keepdims=True)
        return c

    lax.fori_loop(0, db, row, 0)


def _sample_select_kernel(sc_ref, sn_ref, m_ref, ns_ref, *, topk):
    sc = sc_ref[...]
    db, past = sc.shape
    col = lax.broadcasted_iota(I32, (db, past), 1)
    mask, new_sel = _topk_mask(sc, col, topk, extra=sn_ref[:, :1], ncols=past)
    m_ref[...] = mask.astype(F32)
    ns_ref[...] = jnp.broadcast_to(new_sel.astype(F32), ns_ref.shape)


def _sample_attn_kernel(pt_ref, q_ref, m_ref, ns_ref, kn_ref, vn_ref, kc_ref, vc_ref, og_ref, o_ref,
                        kbuf_ref, vbuf_ref, sem_ref, mx_ref, l_ref, acc_ref, *, nh, nkv, hd):
    db, npg = pt_ref.shape
    cp = kbuf_ref.shape[1]
    page = kbuf_ref.shape[2] // nkv
    nchunk = npg // cp
    total = db * nchunk
    g = nh // nkv
    scale = hd ** -0.5

    def page_copies(page_id, slot, pg):
        return (pltpu.make_async_copy(kc_ref.at[page_id], kbuf_ref.at[slot, pg], sem_ref.at[0, slot]),
                pltpu.make_async_copy(vc_ref.at[page_id], vbuf_ref.at[slot, pg], sem_ref.at[1, slot]))

    def issue(gi, slot):
        b = gi // nchunk
        c = gi % nchunk
        for pg in range(cp):
            for cpy in page_copies(pt_ref[b, c * cp + pg], slot, pg):
                cpy.start()

    def wait(slot):
        for pg in range(cp):
            for cpy in page_copies(0, slot, pg):
                cpy.wait()

    def per_kv_head(fn, n):
        head_kv = lax.broadcasted_iota(I32, (nh, n), 0) // g
        out = fn(0)
        for hk in range(1, nkv):
            out = jnp.where(head_kv == hk, fn(hk), out)
        return out

    def update(qb, key_of, val_of, n, keep):
        s = per_kv_head(lambda hk: _nt_dot(qb, key_of(hk)), n) * scale
        s = jnp.where(keep, s, -jnp.inf)
        m_old = mx_ref[...]
        m_new = jnp.maximum(m_old, jnp.max(s, axis=-1, keepdims=True))
        m_safe = jnp.where(m_new == -jnp.inf, 0.0, m_new)
        alpha = jnp.exp(m_old - m_safe)
        p = jnp.exp(s - m_safe)
        pb = p.astype(BF16)
        l_ref[...] = alpha * l_ref[...] + jnp.sum(p, axis=-1, keepdims=True)
        acc_ref[...] = alpha * acc_ref[...] + per_kv_head(lambda hk: _dot(pb, val_of(hk)), hd)
        mx_ref[...] = m_new

    issue(0, 0)

    def step(gi, carry):
        slot = gi % 2
        b = gi // nchunk
        c = gi % nchunk

        @pl.when(gi + 1 < total)
        def _():
            issue(gi + 1, 1 - slot)

        wait(slot)

        @pl.when(c == 0)
        def _():
            mx_ref[...] = jnp.full(mx_ref.shape, -jnp.inf, F32)
            l_ref[...] = jnp.zeros(l_ref.shape, F32)
            acc_ref[...] = jnp.zeros(acc_ref.shape, F32)

        qb = q_ref[b].astype(BF16)
        head_rows = lambda buf, hk: buf[slot, :, pl.ds(hk, page, stride=nkv), :].reshape(cp * page, hd).astype(BF16)
        update(qb, lambda hk: head_rows(kbuf_ref, hk), lambda hk: head_rows(vbuf_ref, hk), cp * page,
               m_ref[b, c] > 0.0)

        @pl.when(c == nchunk - 1)
        def _():
            n8 = kn_ref.shape[1]
            first = lax.broadcasted_iota(I32, (1, n8), 1) == 0
            kn = kn_ref[b].astype(BF16)
            vn = vn_ref[b].astype(BF16)
            update(qb, lambda hk: kn[:, hk * hd:(hk + 1) * hd], lambda hk: vn[:, hk * hd:(hk + 1) * hd], n8,
                   first & (ns_ref[pl.ds(b, 1), :1] > 0.0))
            o = acc_ref[...] / l_ref[...]
            ms = jnp.sum(jnp.sum(o * o, axis=-1, keepdims=True), axis=0, keepdims=True) / (nh * hd)
            o_ref[b] = o * lax.rsqrt(ms + RMS_EPS) * og_ref[...]

        return carry

    lax.fori_loop(0, total, step, 0)


SAMPLE_PAGES_PER_CHUNK = 16


def _sample_attention(q, qi, kw, k, v, og, cache_k, cache_v, cache_kidx, page_table, *, nh, nkv, hd, ih, idim):
    db = q.shape[0]
    npg = page_table.shape[1]
    page = cache_k.shape[1]
    past = npg * page
    topk = min(TOPK_MAX, (past + 1) // 4)
    qi3 = qi.reshape(db, ih, idim)
    w3 = kw[:, idim:idim + ih].reshape(db, ih, 1)
    vmem = pl.BlockSpec(memory_space=pltpu.VMEM)
    smem = pl.BlockSpec(memory_space=pltpu.SMEM)
    hbm = pl.BlockSpec(memory_space=pl.ANY)
    scores, s_new = pl.pallas_call(
        _sample_score_kernel,
        in_specs=[smem, vmem, vmem, vmem, hbm], out_specs=[vmem, vmem],
        out_shape=[jax.ShapeDtypeStruct((db, past), F32), jax.ShapeDtypeStruct((db, LANES), F32)],
        scratch_shapes=[pltpu.VMEM((2, idim, past), F32), pltpu.SemaphoreType.DMA((2,))],
        compiler_params=_cp())(page_table, qi3, w3, kw[:, :idim].reshape(db, 1, idim),
                               cache_kidx.transpose(0, 2, 1))
    mask, new_sel = pl.pallas_call(
        functools.partial(_sample_select_kernel, topk=topk),
        out_shape=[jax.ShapeDtypeStruct((db, past), F32), jax.ShapeDtypeStruct((db, LANES), F32)],
        compiler_params=_cp())(scores, s_new)
    cp = min(SAMPLE_PAGES_PER_CHUNK, npg)
    assert npg % cp == 0
    rep = lambda a: jnp.broadcast_to(a[:, None, :], (db, SUBLANES, a.shape[1]))
    out = pl.pallas_call(
        functools.partial(_sample_attn_kernel, nh=nh, nkv=nkv, hd=hd),
        in_specs=[smem, vmem, vmem, vmem, vmem, vmem, hbm, hbm, vmem], out_specs=vmem,
        out_shape=jax.ShapeDtypeStruct((db, nh, hd), F32),
        scratch_shapes=[pltpu.VMEM((2, cp, page * nkv, hd), F32), pltpu.VMEM((2, cp, page * nkv, hd), F32),
                        pltpu.SemaphoreType.DMA((2, 2)),
                        pltpu.VMEM((nh, 1), F32), pltpu.VMEM((nh, 1), F32), pltpu.VMEM((nh, hd), F32)],
        compiler_params=_cp())(
            page_table, q.astype(F32).reshape(db, nh, hd), mask.reshape(db, npg // cp, 1, cp * page), new_sel,
            rep(k), rep(v), cache_k.reshape(-1, page * nkv, hd), cache_v.reshape(-1, page * nkv, hd),
            og.reshape(nh, hd))
    return out.reshape(db, nh * hd).astype(BF16)


def _ssm_param_kernel(are_ref, aim_ref, ldt_ref, bre_ref, bim_ref, abre_ref, abim_ref, bbre_ref, bbim_ref):
    a_re = are_ref[...]
    a_im = aim_ref[...]
    dt = jnp.exp(ldt_ref[...])
    mag = jnp.exp(a_re * dt)
    ab_re = mag * jnp.cos(a_im * dt)
    ab_im = mag * jnp.sin(a_im * dt)
    den = a_re * a_re + a_im * a_im
    nr = ab_re - 1.0
    co_re = (nr * a_re + ab_im * a_im) / den
    co_im = (ab_im * a_re - nr * a_im) / den
    abre_ref[...] = ab_re
    abim_ref[...] = ab_im
    b_re = bre_ref[...]
    b_im = bim_ref[...]
    bbre_ref[...] = co_re[None] * b_re - co_im[None] * b_im
    bbim_ref[...] = co_re[None] * b_im + co_im[None] * b_re


def _s5_kernel(u_ref, h0re_ref, h0im_ref, ar_ref, ai_ref, wre_ref, wim_ref, cre_ref, cim_ref, d_ref,
               wglu_ref, bglu_ref, og_ref, s_ref, hre_ref, him_ref, bure_ref, buim_ref, y_ref, *, r, tc):
    ci = pl.program_id(0)
    ns, sc_w, sp_w = wre_ref.shape

    @pl.when(ci == 0)
    def _():
        hre_ref[...] = h0re_ref[...]
        him_ref[...] = h0im_ref[...]

    for s in range(ns):
        cols = slice(s * sc_w, (s + 1) * sc_w)
        st = slice(s * sp_w, (s + 1) * sp_w)
        ub = u_ref[:, cols].astype(BF16)
        bure_ref[...] = _dot(ub, wre_ref[s])
        buim_ref[...] = _dot(ub, wim_ref[s])
        a_r = jnp.broadcast_to(ar_ref[:, st], (r, sp_w))
        a_i = jnp.broadcast_to(ai_ref[:, st], (r, sp_w))

        def step(t, carry):
            hr, hi = carry
            rows = pl.ds(pl.multiple_of(t * r, r), r)
            nr = a_r * hr - a_i * hi + bure_ref[rows, :]
            ni = a_r * hi + a_i * hr + buim_ref[rows, :]
            bure_ref[rows, :] = nr
            buim_ref[rows, :] = ni
            return nr, ni

        hr, hi = lax.fori_loop(0, tc, step, (hre_ref[:, st], him_ref[:, st]))
        hre_ref[:, st] = hr
        him_ref[:, st] = hi
        y_ref[:, cols] = (_dot(bure_ref[...].astype(BF16), cre_ref[s])
                          - _dot(buim_ref[...].astype(BF16), cim_ref[s]))
    y = jax.nn.gelu(y_ref[...] + d_ref[...] * u_ref[...])
    z = _dot(y.astype(BF16), wglu_ref[...]) + bglu_ref[...]
    s_ref[...] = _rms(y * jax.nn.sigmoid(z), og_ref[...]).astype(s_ref.dtype)


def _s5(u_tb, h0_re, h0_im, ab_re, ab_im, w_re, w_im, c_re, c_im, d, w_glu, b_glu, og, *, r, tc):
    rows, w = u_tb.shape
    gp = h0_re.shape[1]
    full = lambda a: pl.BlockSpec(a.shape, lambda i: (0,) * a.ndim)
    consts = (h0_re, h0_im, ab_re, ab_im, w_re, w_im, c_re, c_im, d, w_glu, b_glu, og)
    sp_w = w_re.shape[2]
    return pl.pallas_call(
        functools.partial(_s5_kernel, r=r, tc=tc), grid=(rows // (r * tc),),
        in_specs=[pl.BlockSpec((r * tc, w), lambda i: (i, 0))] + [full(c) for c in consts],
        out_specs=[pl.BlockSpec((r * tc, w), lambda i: (i, 0)), full(h0_re), full(h0_im)],
        out_shape=[jax.ShapeDtypeStruct((rows, w), BF16), jax.ShapeDtypeStruct((r, gp), F32),
                   jax.ShapeDtypeStruct((r, gp), F32)],
        scratch_shapes=[pltpu.VMEM((r * tc, sp_w), F32), pltpu.VMEM((r * tc, sp_w), F32),
                        pltpu.VMEM((r * tc, w), F32)],
        compiler_params=_cp("arbitrary"))(u_tb, *consts)


def _out_proj_kernel(x_ref, g1_ref, sh_ref, sc_ref, a_ref, s_ref, wa_ref, ws_ref, n2_ref, x1_ref, h2_ref):
    mix = _dot(a_ref[...], wa_ref[...]) + _dot(s_ref[...], ws_ref[...])
    x1 = x_ref[...] + g1_ref[0] * mix
    x1_ref[...] = x1
    h2_ref[...] = _rms(x1, n2_ref[...]) * (1.0 + sc_ref[0]) + sh_ref[0]


def _out_proj(x, gate1, shift2, scale2, a_n, s_n, w_a, w_s, n2, *, tm, rows_per_batch):
    t, d = x.shape
    full = lambda a: pl.BlockSpec(a.shape, lambda i: (0,) * a.ndim)
    rowblk = lambda n: pl.BlockSpec((tm, n), lambda i: (i, 0))
    return pl.pallas_call(
        _out_proj_kernel, grid=(t // tm,),
        in_specs=[rowblk(d)] + [_mod_spec(m, tm, rows_per_batch) for m in (gate1, shift2, scale2)]
        + [rowblk(a_n.shape[1]), rowblk(s_n.shape[1]), full(w_a), full(w_s), full(n2)],
        out_specs=[rowblk(d), rowblk(d)],
        out_shape=[jax.ShapeDtypeStruct((t, d), F32), jax.ShapeDtypeStruct((t, d), F32)],
        compiler_params=_cp("parallel"))(x, gate1, shift2, scale2, a_n, s_n, w_a, w_s, n2)


def _take_top(s, order, k, pay=None):
    big = 3.0e38
    kid = lax.broadcasted_iota(I32, (k, s.shape[1]), 0)
    vals = jnp.zeros((k, s.shape[1]), F32)
    picks = jnp.zeros((k, s.shape[1]), F32)
    for it in range(k):
        m = jnp.max(s, axis=0, keepdims=True)
        first = jnp.min(jnp.where(s == m, order, big), axis=0, keepdims=True)
        hit = order == first
        pv = first if pay is None else jnp.max(jnp.where(hit, pay, -1.0), axis=0, keepdims=True)
        vals = jnp.where(kid == it, m, vals)
        picks = jnp.where(kid == it, pv, picks)
        s = jnp.where(hit, -jnp.inf, s)
    return vals, picks


def _pair_candidates(v1, i1, v2, i2, nk):
    k, tb = v1.shape
    assert k == 16
    widths = [k] + [SUBLANES] * (k // 2 - 1)
    row = lambda w: lax.broadcasted_iota(I32, (w, tb), 0).astype(F32)
    sums = [v1[a:a + 1, :] + v2[:w, :] for a, w in enumerate(widths)]
    flat = [row(w) + float(a * k) for a, w in enumerate(widths)]
    ids = [i1[a:a + 1, :] * float(nk) + i2[:w, :] for a, w in enumerate(widths)]
    sums.append(v1[k // 2:, :] + v2[:1, :])
    flat.append((row(k // 2) + float(k // 2)) * float(k))
    ids.append(i1[k // 2:, :] * float(nk) + i2[:1, :])
    return jnp.concatenate(sums, axis=0), jnp.concatenate(flat, axis=0), jnp.concatenate(ids, axis=0)


def _peer_route_kernel(h_ref, wq_ref, k1_ref, k2_ref, e_ref, g_ref, *, nk):
    half = k1_ref.shape[2]
    tb = h_ref.shape[0]
    k = PEER_TOPK
    qt = _nt_dot(wq_ref[...], h_ref[...].astype(BF16))
    s1 = _dot(k1_ref[0], qt[:half, :].astype(BF16))
    s2 = _dot(k2_ref[0], qt[half:, :].astype(BF16))
    rid = lax.broadcasted_iota(I32, (nk, tb), 0).astype(F32)
    v1, i1 = _take_top(s1, rid, k)
    v2, i2 = _take_top(s2, rid, k)
    cand, flat, ids = _pair_candidates(v1, i1, v2, i2, nk)
    sc, e = _take_top(cand, flat, k, pay=ids)
    p = jnp.exp(sc - jnp.max(sc, axis=0, keepdims=True))
    g_ref[0] = p / jnp.sum(p, axis=0, keepdims=True)
    e_ref[0] = e.astype(I32)


def _peer_route(h2, wq_t, k1, k2, *, tb):
    t, d = h2.shape
    hp, nk, half = k1.shape
    return pl.pallas_call(
        functools.partial(_peer_route_kernel, nk=nk), grid=(t // tb, hp),
        in_specs=[pl.BlockSpec((tb, d), lambda i, h: (i, 0)), pl.BlockSpec((2 * half, d), lambda i, h: (h, 0)),
                  pl.BlockSpec((1, nk, half), lambda i, h: (h, 0, 0)),
                  pl.BlockSpec((1, nk, half), lambda i, h: (h, 0, 0))],
        out_specs=[pl.BlockSpec((1, PEER_TOPK, tb), lambda i, h: (h, 0, i)),
                   pl.BlockSpec((1, PEER_TOPK, tb), lambda i, h: (h, 0, i))],
        out_shape=[jax.ShapeDtypeStruct((hp, PEER_TOPK, t), I32), jax.ShapeDtypeStruct((hp, PEER_TOPK, t), F32)],
        compiler_params=_cp("parallel", "arbitrary"))(h2, wq_t, k1, k2)


PEER_GATHER_BATCH = 4
PEER_RING_GROUPS = 4
PEER_RING_AHEAD = PEER_RING_GROUPS - 1


def _peer_gather_kernel(e_ref, en_ref, h_ref, g_ref, x1_ref, gate_ref, tab_ref, y_ref, *scratch):
    *bufs, wgt_ref, sem_ref = scratch
    nbt = PEER_GATHER_BATCH
    ngrp = len(bufs) // nbt
    ahead = PEER_RING_AHEAD
    tb = h_ref.shape[0]
    npick = bufs[0].shape[0]
    R = bufs[0].shape[1] // 2
    i = pl.program_id(0)

    def slab_copy(e, slot, p):
        return pltpu.make_async_copy(tab_ref.at[e], bufs[slot].at[p], sem_ref.at[slot])

    def issue(idx_ref, j, slot):
        for p in range(npick):
            slab_copy(idx_ref[j, p], slot, p).start(priority=p % 2)

    def wait(slot):
        pltpu.make_async_copy(tab_ref.at[pl.ds(0, npick)], bufs[slot], sem_ref.at[slot]).wait()

    group_sum = (lax.broadcasted_iota(I32, (npick, npick * R), 1) // R
                 == lax.broadcasted_iota(I32, (npick, npick * R), 0)).astype(BF16)
    gw = g_ref.shape[1]
    g_lane = lax.broadcasted_iota(I32, (npick, gw), 1)

    def compute(j, slot):
        buf = bufs[slot]
        xb = h_ref[j]
        prods = [buf[p, :R, :] * xb for p in range(npick)]
        hid = jnp.sum(_dot(group_sum, jnp.concatenate(prods, axis=0)), axis=-1, keepdims=True)
        jcol = (i * tb) % gw + j
        gcol = jnp.sum(jnp.where(g_lane == jcol, g_ref[...], 0.0), axis=-1, keepdims=True)
        wgt_ref[slot % nbt] = jnp.broadcast_to(gcol * jax.nn.gelu(hid), (npick, LANES))
        nacc = 4
        acc = [jnp.zeros((R, LANES), F32) for _ in range(nacc)]
        for p in range(npick):
            w_p = jnp.broadcast_to(wgt_ref[slot % nbt, p:p + 1, :], (R, LANES))
            acc[p % nacc] = acc[p % nacc] + w_p * buf[p, R:, :].astype(F32)
        out = sum(acc[1:], acc[0])
        gate = gate_ref[0, 0] if gate_ref.shape[1] == 1 else gate_ref[0, j]
        y_ref[j] = x1_ref[j] + gate * out

    @pl.when(i == 0)
    def _():
        for s in range(ahead * nbt):
            issue(e_ref, s, s)

    def ring_turn(k, c):
        for grp in range(ngrp):
            j0 = (ngrp * k + grp) * nbt
            for s in range(nbt):
                wait(grp * nbt + s)
            for s in range(nbt):
                issue(en_ref, j0 + s, ((grp + ahead) % ngrp) * nbt + s)
            for s in range(nbt):
                compute(j0 + s, grp * nbt + s)
        return c

    lax.fori_loop(0, tb // (ngrp * nbt), ring_turn, 0)

    @pl.when(i == pl.num_programs(0) - 1)
    def _():
        for s in range(ahead * nbt):
            wait(s)


def _peer_gather(e_tok, h2, g_t, x1, gate2, table, *, tb, rows_per_batch):
    t, d = h2.shape
    npick = e_tok.shape[1]
    rows = table.shape[1] // 2
    assert rows * LANES == d
    nbt = PEER_GATHER_BATCH
    nbuf = PEER_RING_GROUPS * nbt
    assert tb % nbuf == 0 and t % tb == 0
    slab = lambda a: a.reshape(*a.shape[:-1], rows, LANES)
    gate = slab(gate2)
    if gate.shape[1] == 1:
        gate_spec = pl.BlockSpec((1, 1, rows, LANES), lambda i: ((i * tb) // rows_per_batch, 0, 0, 0))
    else:
        assert gate.shape[1] == tb
        gate_spec = pl.BlockSpec((1, tb, rows, LANES), lambda i: (0, 0, 0, 0))
    tokblk = pl.BlockSpec((tb, rows, LANES), lambda i: (i, 0, 0))
    idxblk = pl.BlockSpec((tb, npick), lambda i: (i, 0), memory_space=pltpu.SMEM)
    gw = t if t < LANES else max(tb, LANES)
    assert gw % tb == 0 and t % gw == 0
    y = pl.pallas_call(
        _peer_gather_kernel, grid=(t // tb,),
        in_specs=[idxblk, idxblk, tokblk, pl.BlockSpec((npick, gw), lambda i: (0, (i * tb) // gw)), tokblk,
                  gate_spec, pl.BlockSpec(memory_space=pl.ANY)],
        out_specs=tokblk, out_shape=jax.ShapeDtypeStruct((t, rows, LANES), F32),
        scratch_shapes=[pltpu.VMEM((npick, 2 * rows, LANES), BF16)] * nbuf
        + [pltpu.VMEM((nbt, npick, LANES), F32), pltpu.SemaphoreType.DMA((nbuf,))],
        compiler_params=_cp("arbitrary", disable_bounds_checks=True))(
            e_tok, jnp.roll(e_tok, -PEER_RING_AHEAD * nbt, axis=0), slab(h2.astype(BF16)), g_t, slab(x1), gate, table)
    return y.reshape(t, d)


def _expert_slabs(u, v):
    e, d = u.shape
    r = d // LANES
    assert r * LANES == d and r % (2 * SUBLANES) == 0
    return jnp.concatenate([u.astype(BF16).reshape(e, r, LANES), v.astype(BF16).reshape(e, r, LANES)], axis=1)


def _block_diag(m, ns):
    g, a, b = m.shape
    gs = g // ns
    eye = jnp.eye(gs, dtype=m.dtype)
    return jnp.einsum('sgab,gh->sgahb', m.reshape(ns, gs, a, b), eye).reshape(ns, gs * a, gs * b)


def _trunk(x, mods, wts, ssm, peer, h0_re, h0_im, attend, *, nb, seq, dims):
    t, d = x.shape
    shift1, scale1, gate1, shift2, scale2, gate2 = mods
    per_batch_mod = shift1.shape[1] == 1
    tm = min(256, seq) if per_batch_mod else t
    big_tm = min(512, seq) if per_batch_mod else t
    q, k, v = _row_call(functools.partial(_proj_qkv_kernel, hd=dims['hd']), big_tm, seq, x, (shift1, scale1),
                        (wts['norm1_g'], wts['w_q'], wts['w_k'], wts['w_v'], wts['q_gain'], wts['k_gain']),
                        ((dims['a'], BF16), (dims['kv'], F32), (dims['kv'], F32)))
    qi, kw, u = _row_call(_proj_idx_kernel, big_tm, seq, x, (shift1, scale1),
                          (wts['norm1_g'], wts['w_qi'], wts['w_kw'], wts['w_u']),
                          ((dims['ih'] * dims['idim'], BF16), (LANES, F32), (dims['w'], F32)))
    a_n = attend(q, qi, kw, k, v)
    w = dims['w']
    u_tb = u.reshape(nb, seq, w).transpose(1, 0, 2).reshape(t, w)
    tc = min(64, seq)
    s_tb, h_re, h_im = _s5(u_tb, h0_re, h0_im, *ssm, r=nb, tc=tc)
    s_n = s_tb.reshape(seq, nb, w).transpose(1, 0, 2).reshape(t, w)
    x1, h2 = _out_proj(x, gate1, shift2, scale2, a_n, s_n, wts['w_out_a'], wts['w_out_s'], wts['norm2_g'],
                       tm=tm, rows_per_batch=seq)
    e, g = _peer_route(h2, peer['wq_t'], peer['k1'], peer['k2'], tb=tm)
    npick = e.shape[0] * e.shape[1]
    tb = min(64, tm)
    y = _peer_gather(e.reshape(npick, t).T, h2, g.reshape(npick, t), x1, gate2, peer['table'],
                     tb=tb, rows_per_batch=seq)
    return y, k, v, kw[:, :dims['idim']], h_re, h_im


def kernel(x_prompt, x_sample, cache_k, cache_v, cache_kidx, state_ssm_re, state_ssm_im, page_table, c_prompt, c_sample, w_ada, b_ada, norm1_g, w_in, q_gain, k_gain, ssm_A_re, ssm_A_im, ssm_B_re, ssm_B_im, ssm_C_re, ssm_C_im, ssm_D, ssm_log_dt, ssm_w_glu, ssm_b_glu, attn_out_g, ssm_out_g, w_out, norm2_g, peer_w_q, peer_sub_keys1, peer_sub_keys2, peer_u, peer_v):
    b, seq, d = x_prompt.shape
    db, dseq, _ = x_sample.shape
    assert dseq == 1
    hd = q_gain.shape[0]
    nkv = cache_k.shape[2]
    idim = cache_kidx.shape[2]
    g, p, c = ssm_B_re.shape
    w = g * c
    a = d - w
    nh = a // hd
    kv = nkv * hd
    ih = (w_in.shape[1] - a - 2 * kv - idim - w) // (idim + 1)
    assert a + 2 * kv + ih * idim + idim + ih + w == w_in.shape[1] and idim + ih <= LANES
    dims = dict(hd=hd, a=a, kv=kv, ih=ih, idim=idim, w=w)

    c_all = jnp.concatenate([c_prompt, c_sample], axis=0)
    pad = (-c_all.shape[0]) % SUBLANES
    c_all = jnp.pad(c_all, ((0, pad), (0, 0)))
    m = _adaln(c_all, w_ada, b_ada)
    mods_p = [mm[:b].reshape(b, 1, d) for mm in jnp.split(m, 6, axis=-1)]
    mods_s = [mm[b:b + db].reshape(1, db, d) for mm in jnp.split(m, 6, axis=-1)]

    cuts = [a, a + kv, a + 2 * kv, a + 2 * kv + ih * idim, a + 2 * kv + ih * idim + idim + ih]
    w_q, w_k, w_v, w_qi, w_kw, w_u = [x.astype(BF16) for x in jnp.split(w_in, cuts, axis=1)]
    w_kw = jnp.pad(w_kw, ((0, 0), (0, LANES - idim - ih)))
    row = lambda x: x.reshape(1, -1)
    wts = dict(norm1_g=row(norm1_g), w_q=w_q, w_k=w_k, w_v=w_v, w_qi=w_qi, w_kw=w_kw, w_u=w_u,
               q_gain=row(q_gain), k_gain=row(k_gain), w_out_a=w_out[:a].astype(BF16),
               w_out_s=w_out[a:].astype(BF16), norm2_g=row(norm2_g))

    gp_shape = jax.ShapeDtypeStruct((g, p), F32)
    cgp_shape = jax.ShapeDtypeStruct((c, g, p), F32)
    ab_re, ab_im, bb_re, bb_im = pl.pallas_call(
        _ssm_param_kernel, out_shape=[gp_shape, gp_shape, cgp_shape, cgp_shape], compiler_params=_cp())(
            ssm_A_re, ssm_A_im, ssm_log_dt.reshape(g, 1), ssm_B_re.transpose(2, 0, 1), ssm_B_im.transpose(2, 0, 1))
    ns = max(1, g // SSM_SLAB_GROUPS)
    ssm = (ab_re.reshape(1, g * p), ab_im.reshape(1, g * p),
           _block_diag(bb_re.transpose(1, 0, 2), ns).astype(BF16), _block_diag(bb_im.transpose(1, 0, 2), ns).astype(BF16),
           _block_diag(ssm_C_re.transpose(0, 2, 1), ns).astype(BF16),
           _block_diag(ssm_C_im.transpose(0, 2, 1), ns).astype(BF16),
           ssm_D.reshape(1, w), ssm_w_glu.astype(BF16), row(ssm_b_glu), row(ssm_out_g))

    hp, nk, half = peer_sub_keys1.shape
    peer = dict(wq_t=peer_w_q.T.astype(BF16), k1=peer_sub_keys1.astype(BF16), k2=peer_sub_keys2.astype(BF16),
                table=_expert_slabs(peer_u, peer_v))
    og = row(attn_out_g)

    attend_p = functools.partial(_prompt_attention, og=og, b=b, nh=nh, nkv=nkv, hd=hd, ih=ih, idim=idim)
    attend_s = functools.partial(_sample_attention, og=og, cache_k=cache_k, cache_v=cache_v, cache_kidx=cache_kidx,
                                 page_table=page_table, nh=nh, nkv=nkv, hd=hd, ih=ih, idim=idim)
    zeros = jnp.zeros((b, g * p), F32)
    yp, k_p, v_p, ki_p, re_p, im_p = _trunk(x_prompt.reshape(b * seq, d), mods_p, wts, ssm, peer, zeros, zeros,
                                            attend_p, nb=b, seq=seq, dims=dims)
    ys, k_s, v_s, ki_s, re_s, im_s = _trunk(x_sample.reshape(db, d), mods_s, wts, ssm, peer,
                                            state_ssm_re.reshape(db, g * p), state_ssm_im.reshape(db, g * p),
                                            attend_s, nb=db, seq=1, dims=dims)
    return (yp.reshape(b, seq, d), ys.reshape(db, 1, d),
            k_p.reshape(b, seq, nkv, hd), v_p.reshape(b, seq, nkv, hd), ki_p.reshape(b, seq, idim),
            re_p.reshape(b, g, p), im_p.reshape(b, g, p),
            k_s.reshape(db, 1, nkv, hd), v_s.reshape(db, 1, nkv, hd), ki_s.reshape(db, 1, idim),
            re_s.reshape(db, g, p), im_s.reshape(db, g, p))
```
